```python
import math
import jax
import jax.numpy as jnp
from jax import lax
import numpy as np

D_MODEL = 1024
BATCH = 8
SEQ = 8192
DEPTH = 1

GRID_W = 64
CTX_LEN = 256
N_MOD = 6
RMS_EPS = 1e-6
NA_HEADS = 16
HEAD_DIM = 64
D_A = NA_HEADS * HEAD_DIM
WIN_ROWS = 8
WIN_COLS = 16
D_B = 1024
SHORT_CONV = 3
EMB_DIM = 33
FILT_HID = 64
DECAY_TARGET = 1e-2
FAST_DECAY_PCT = 0.3
SLOW_DECAY_PCT = 1.5
N_EXPERTS = 32
TOP_K = 4
D_FF = 1024
SWIGLU_LIMIT = 7.0
SWIGLU_ALPHA = 1.702
EXPERT_GROUP = 512
Q_OFF = 0
K_OFF = D_A
V_OFF = 2 * D_A
U_OFF = 3 * D_A
GA_OFF = U_OFF + 3 * D_B
GB_OFF = GA_OFF + D_MODEL
IN_WIDTH = GB_OFF + D_MODEL

kernel_name = 'hybrid_natten_hyena_moe_dit'


def rms_norm(x, g):
    xf = x.astype(jnp.float32)
    y = xf * lax.rsqrt(jnp.mean(xf * xf, axis=-1, keepdims=True) + RMS_EPS)
    return (y * g.astype(jnp.float32)).astype(x.dtype)


def modulate(h, shift, scale):
    return h * (1.0 + scale) + shift


def cols(h, w, off, width):
    return h @ w[:, off:off + width]


def heads(t):
    return t.reshape(t.shape[:-1] + (NA_HEADS, HEAD_DIM))


def qk_norm(t, g):
    return rms_norm(heads(t), g)


def neighbourhood_attention(q, k, v, k_ctx, v_ctx, rpb):
    b, s, h, dh = q.shape
    rows = s // GRID_W
    wh = min(WIN_ROWS, rows)
    scale = dh ** -0.5
    r_ids = jnp.arange(rows, dtype=jnp.int32)
    row_start = jnp.clip(r_ids - wh // 2, 0, rows - wh)
    c_ids = jnp.arange(GRID_W, dtype=jnp.int32)
    col_start = jnp.clip(c_ids - WIN_COLS // 2, 0, GRID_W - WIN_COLS)
    col_idx = col_start[:, None] + jnp.arange(WIN_COLS, dtype=jnp.int32)[None, :]
    dcol = col_idx - c_ids[:, None] + (WIN_COLS - 1)
    qg = q.reshape(b, rows, GRID_W, h, dh)
    kg = k.reshape(b, rows, GRID_W, h, dh)
    vg = v.reshape(b, rows, GRID_W, h, dh)
    n_win = wh * WIN_COLS

    def one_row(args):
        r, rs = args
        q_r = lax.dynamic_index_in_dim(qg, r, axis=1, keepdims=False)
        k_win = lax.dynamic_slice_in_dim(kg, rs, wh, axis=1)[:, :, col_idx]
        v_win = lax.dynamic_slice_in_dim(vg, rs, wh, axis=1)[:, :, col_idx]
        drow = rs + jnp.arange(wh, dtype=jnp.int32) - r + (WIN_ROWS - 1)
        bias = rpb[:, drow[None, :, None], dcol[:, None, :]].astype(jnp.float32)
        s_win = jnp.einsum('bchd,bicjhd->bhcij', q_r, k_win).astype(jnp.float32) * scale + bias
        s_ctx = jnp.einsum('bchd,bnhd->bhcn', q_r, k_ctx).astype(jnp.float32) * scale
        logits = jnp.concatenate([s_win.reshape(b, h, GRID_W, n_win), s_ctx], axis=-1)
        p = jax.nn.softmax(logits, axis=-1).astype(v.dtype)
        o = jnp.einsum('bhcij,bicjhd->bchd', p[..., :n_win].reshape(b, h, GRID_W, wh, WIN_COLS), v_win)
        return o + jnp.einsum('bhcn,bnhd->bchd', p[..., n_win:], v_ctx)

    out = lax.map(one_row, (r_ids, row_start))
    return out.transpose(1, 0, 2, 3, 4).reshape(b, s, h * dh)


def context_attention(q, k, v):
    b, n, h, dh = q.shape
    logits = jnp.einsum('bqhd,bkhd->bhqk', q, k).astype(jnp.float32) * (dh ** -0.5)
    p = jax.nn.softmax(logits, axis=-1).astype(v.dtype)
    return jnp.einsum('bhqk,bkhd->bqhd', p, v).reshape(b, n, h * dh)


def short_conv(u, w, b):
    L = u.shape[1]
    up = jnp.pad(u, ((0, 0), (1, 1), (0, 0)))
    return up[:, :L] * w[0] + up[:, 1:L + 1] * w[1] + up[:, 2:] * w[2] + b


def implicit_filter(L, w1, b1, fr1, w2, b2, fr2, w3):
    f32 = jnp.float32
    t = jnp.linspace(0.0, 1.0, L, dtype=f32)[:, None]
    bands = (EMB_DIM - 1) // 2
    ang = (2.0 * math.pi / L) * jnp.arange(L, dtype=f32)[:, None]
    freqs = jnp.linspace(1e-4, bands - 1, bands, dtype=f32)[None, :]
    feat = jnp.concatenate([t, jnp.cos(freqs * ang), -jnp.sin(freqs * ang)], axis=-1)
    hid = jnp.sin(fr1.astype(f32) * (feat @ w1.astype(f32) + b1.astype(f32)))
    hid = jnp.sin(fr2.astype(f32) * (hid @ w2.astype(f32) + b2.astype(f32)))
    filt = hid @ w3.astype(f32)
    deltas = jnp.abs(jnp.linspace(math.log(DECAY_TARGET) / SLOW_DECAY_PCT,
                                  math.log(DECAY_TARGET) / FAST_DECAY_PCT, D_B, dtype=f32))
    filt = filt * jnp.exp(-t * jnp.tile(deltas, 2)[None, :])
    return filt[:, :D_B], filt[:, D_B:]


def hyena(u3, conv_w, conv_b, w1, b1, fr1, w2, b2, fr2, w3, skip):
    f32 = jnp.float32
    L = u3.shape[1]
    u3 = short_conv(u3, conv_w, conv_b)
    v, x1, x0 = jnp.split(u3, 3, axis=-1)
    h_fwd, h_bwd = implicit_filter(L, w1, b1, fr1, w2, b2, fr2, w3)
    k_full = jnp.concatenate([h_fwd, jnp.zeros((1, D_B), f32), h_bwd[:0:-1]], axis=0)
    z = (v * x1).astype(f32)
    zf = jnp.fft.rfft(z, n=2 * L, axis=1)
    kf = jnp.fft.rfft(k_full, axis=0)
    y = jnp.fft.irfft(zf * kf[None], n=2 * L, axis=1)[:, :L]
    y = y + z * skip.astype(f32)
    return (x0.astype(f32) * y).astype(u3.dtype)


def merge_branches(o_a, o_b, h, w_in_l, w_ba, w_bb, w_out):
    g_a = jax.nn.sigmoid(cols(h, w_in_l, GA_OFF, D_MODEL))
    g_b = jax.nn.sigmoid(cols(h, w_in_l, GB_OFF, D_MODEL))
    return (g_a * (o_a @ w_ba) + g_b * (o_b @ w_bb)) @ w_out


def routed_ffn(h, w_r, b_r, w_gu, b_gu, w_d, b_d):
    f32 = jnp.float32
    n = h.shape[0]
    logits = (h @ w_r + b_r).astype(f32)
    top_val, top_idx = lax.top_k(logits, TOP_K)
    top_w = jax.nn.softmax(top_val, axis=-1)
    n_assign = n * TOP_K
    e_flat = top_idx.reshape(-1)
    tok_flat = jnp.repeat(jnp.arange(n, dtype=jnp.int32), TOP_K)
    w_flat = top_w.reshape(-1)
    order = jnp.argsort(e_flat)
    e_sorted = e_flat[order]
    counts = jnp.bincount(e_flat, length=N_EXPERTS)
    padded = (counts + EXPERT_GROUP - 1) // EXPERT_GROUP * EXPERT_GROUP
    pad_end = jnp.cumsum(padded)
    pad_start = pad_end - padded
    start = jnp.cumsum(counts) - counts
    dest = pad_start[e_sorted] + jnp.arange(n_assign, dtype=jnp.int32) - start[e_sorted]
    n_groups = -(-n_assign // EXPERT_GROUP) + N_EXPERTS
    slot_tok = jnp.zeros((n_groups * EXPERT_GROUP,), jnp.int32).at[dest].set(tok_flat[order])
    slot_w = jnp.zeros((n_groups * EXPERT_GROUP,), f32).at[dest].set(w_flat[order])
    group_expert = jnp.minimum(
        jnp.searchsorted(pad_end, jnp.arange(n_groups, dtype=jnp.int32) * EXPERT_GROUP, side='right'),
        N_EXPERTS - 1)

    def group_step(y, xs):
        tok, wt, e = xs
        xg = h[tok]
        gu = xg @ w_gu[e] + b_gu[e]
        gate = jnp.minimum(gu[:, :D_FF], SWIGLU_LIMIT)
        up = jnp.clip(gu[:, D_FF:], -SWIGLU_LIMIT, SWIGLU_LIMIT)
        act = (up + 1.0) * gate * jax.nn.sigmoid(SWIGLU_ALPHA * gate)
        out = (act @ w_d[e] + b_d[e]).astype(f32) * wt[:, None]
        return y.at[tok].add(out.astype(y.dtype)), None

    y, _ = lax.scan(group_step, jnp.zeros_like(h),
                    (slot_tok.reshape(n_groups, EXPERT_GROUP), slot_w.reshape(n_groups, EXPERT_GROUP), group_expert))
    return y


def setup_inputs(seed: int = 0) -> dict:
    key = jax.random.key(seed)
    ks = jax.random.split(key, 31)

    def nrm(k, shape, s):
        return s * jax.random.normal(k, shape, jnp.float32)

    D, L_, E = D_MODEL, DEPTH, N_EXPERTS
    return {
        'x': nrm(ks[0], (BATCH, SEQ, D), 1.0),
        'c': nrm(ks[1], (BATCH, D), 1.0),
        'ctx': nrm(ks[2], (BATCH, CTX_LEN, D), 1.0),
        'c_ctx': nrm(ks[3], (D,), 1.0),
        'w_ada': nrm(ks[4], (L_, D, N_MOD * D), 0.5 * D ** -0.5),
        'b_ada': nrm(ks[5], (L_, N_MOD * D), 0.02),
        'norm1_g': 1.0 + nrm(ks[6], (L_, D), 0.02),
        'norm2_g': 1.0 + nrm(ks[7], (L_, D), 0.02),
        'w_in': nrm(ks[8], (L_, D, IN_WIDTH), D ** -0.5),
        'q_norm_g': 1.0 + nrm(ks[9], (L_, HEAD_DIM), 0.02),
        'k_norm_g': 1.0 + nrm(ks[10], (L_, HEAD_DIM), 0.02),
        'rpb': nrm(ks[11], (L_, NA_HEADS, 2 * WIN_ROWS - 1, 2 * WIN_COLS - 1), 0.05),
        'conv_w': nrm(ks[12], (L_, SHORT_CONV, 3 * D_B), SHORT_CONV ** -0.5),
        'conv_b': nrm(ks[13], (L_, 3 * D_B), 0.02),
        'filt_w1': nrm(ks[14], (L_, EMB_DIM, FILT_HID), EMB_DIM ** -0.5),
        'filt_b1': nrm(ks[15], (L_, FILT_HID), 0.02),
        'filt_freq1': 1.0 + nrm(ks[16], (L_, FILT_HID), 0.02),
        'filt_w2': nrm(ks[17], (L_, FILT_HID, FILT_HID), FILT_HID ** -0.5),
        'filt_b2': nrm(ks[18], (L_, FILT_HID), 0.02),
        'filt_freq2': 1.0 + nrm(ks[19], (L_, FILT_HID), 0.02),
        'filt_w3': nrm(ks[20], (L_, FILT_HID, 2 * D_B), 0.02 * FILT_HID ** -0.5),
        'hyena_skip': nrm(ks[21], (L_, D_B), 1.0),
        'w_branch_a': nrm(ks[22], (L_, D_A, D), D_A ** -0.5),
        'w_branch_b': nrm(ks[23], (L_, D_B, D), D_B ** -0.5),
        'w_out': nrm(ks[24], (L_, D, D), D ** -0.5),
        'w_router': nrm(ks[25], (L_, D, E), D ** -0.5),
        'b_router': nrm(ks[26], (L_, E), 0.01),
        'w_gate_up': nrm(ks[27], (L_, E, D, 2 * D_FF), D ** -0.5),
        'b_gate_up': nrm(ks[28], (L_, E, 2 * D_FF), 0.02),
        'w_down': nrm(ks[29], (L_, E, D_FF, D), D_FF ** -0.5),
        'b_down': nrm(ks[30], (L_, E, D), 0.02),
    }


def reference(x, c, ctx, c_ctx, w_ada, b_ada, norm1_g, norm2_g, w_in, q_norm_g, k_norm_g, rpb,
              conv_w, conv_b, filt_w1, filt_b1, filt_freq1, filt_w2, filt_b2, filt_freq2, filt_w3,
              hyena_skip, w_branch_a, w_branch_b, w_out, w_router, b_router, w_gate_up, b_gate_up,
              w_down, b_down):
    b, s, d = x.shape
    for l in range(DEPTH):
        mod = jax.nn.silu(c) @ w_ada[l] + b_ada[l]
        sh1, sc1, g1, sh2, sc2, g2 = [m[:, None, :] for m in jnp.split(mod, N_MOD, axis=-1)]
        mod_c = jax.nn.silu(c_ctx) @ w_ada[l] + b_ada[l]
        sh1c, sc1c, g1c, sh2c, sc2c, g2c = jnp.split(mod_c, N_MOD)
        w_in_l = w_in[l]
        filt = (filt_w1[l], filt_b1[l], filt_freq1[l], filt_w2[l], filt_b2[l], filt_freq2[l], filt_w3[l])

        hc = modulate(rms_norm(ctx, norm1_g[l]), sh1c, sc1c)
        kc = qk_norm(cols(hc, w_in_l, K_OFF, D_A), k_norm_g[l])
        vc = heads(cols(hc, w_in_l, V_OFF, D_A))

        h = modulate(rms_norm(x, norm1_g[l]), sh1, sc1)
        q = qk_norm(cols(h, w_in_l, Q_OFF, D_A), q_norm_g[l])
        k = qk_norm(cols(h, w_in_l, K_OFF, D_A), k_norm_g[l])
        v = heads(cols(h, w_in_l, V_OFF, D_A))
        o_a = neighbourhood_attention(q, k, v, kc, vc, rpb[l])
        o_b = hyena(cols(h, w_in_l, U_OFF, 3 * D_B), conv_w[l], conv_b[l], *filt, hyena_skip[l])
        x = x + g1 * merge_branches(o_a, o_b, h, w_in_l, w_branch_a[l], w_branch_b[l], w_out[l])

        h2 = modulate(rms_norm(x, norm2_g[l]), sh2, sc2)
        x = x + g2 * routed_ffn(h2.reshape(b * s, d), w_router[l], b_router[l], w_gate_up[l],
                                b_gate_up[l], w_down[l], b_down[l]).reshape(b, s, d)

        if l < DEPTH - 1:
            qc = qk_norm(cols(hc, w_in_l, Q_OFF, D_A), q_norm_g[l])
            oc_a = context_attention(qc, kc, vc)
            oc_b = hyena(cols(hc, w_in_l, U_OFF, 3 * D_B), conv_w[l], conv_b[l], *filt, hyena_skip[l])
            ctx = ctx + g1c * merge_branches(oc_a, oc_b, hc, w_in_l, w_branch_a[l], w_branch_b[l], w_out[l])
            h2c = modulate(rms_norm(ctx, norm2_g[l]), sh2c, sc2c)
            ctx = ctx + g2c * routed_ffn(h2c.reshape(-1, d), w_router[l], b_router[l], w_gate_up[l],
                                         b_gate_up[l], w_down[l], b_down[l]).reshape(ctx.shape)
    return x
```

```python
import functools
import math

import numpy as np
import jax
import jax.numpy as jnp
from jax import lax
from jax.experimental import pallas as pl
from jax.experimental.pallas import tpu as pltpu

f32, bf16, i32 = jnp.float32, jnp.bfloat16, jnp.int32
HI = lax.Precision.HIGHEST

LANES = 128
D_MODEL = 1024
N_HEADS = 16
HEAD_DIM = 64
N_PAIRS = N_HEADS // 2
GRID_W = 64
WIN_ROWS = 8
WIN_COLS = 16
CTX_LEN = 256
N_MOD = 6
RMS_EPS = 1e-6
D_B = 1024
EMB_DIM = 33
FILT_HID = 64
N_BANDS = (EMB_DIM - 1) // 2
DECAY_TARGET = 1e-2
FAST_DECAY_PCT = 0.3
SLOW_DECAY_PCT = 1.5
N_EXPERTS = 32
TOP_K = 4
D_FF = 1024
SWIGLU_LIMIT = 7.0
SWIGLU_ALPHA = 1.702
NEG_BIG = -1e30

QROWS = 8
KROWS = 16
SLAB_ROWS = 10
N_SLAB_TYPES = 5
VMEM_LIMIT = 56 * 1024 * 1024


def _cparams(ndim, vmem=VMEM_LIMIT):
    return pltpu.CompilerParams(dimension_semantics=("arbitrary",) * ndim, vmem_limit_bytes=vmem)


def _sigmoid(x):
    return 1.0 / (1.0 + jnp.exp(-x))


def _dot(a, b, precision=None):
    return jnp.dot(a, b, precision=precision, preferred_element_type=f32)


def _dot_nt(a, b, precision=None):
    return lax.dot_general(a, b, (((1,), (1,)), ((), ())), precision=precision, preferred_element_type=f32)


def _dot_tn(a, b):
    return lax.dot_general(a, b, (((0,), (0,)), ((), ())), preferred_element_type=f32)


def _bdot(a, b, precision=None):
    return lax.dot_general(a, b, (((2,), (1,)), ((0,), (0,))), precision=precision, preferred_element_type=f32)


def _adaln_kernel(cv_ref, w_ref, b_ref, o_ref):
    cv = cv_ref[...]
    o_ref[...] = _dot(cv * _sigmoid(cv), w_ref[...], HI) + b_ref[...]


def _adaln(cv, w, b):
    rows, d = cv.shape
    n = w.shape[1]
    tn = 1024
    return pl.pallas_call(
        _adaln_kernel,
        grid=(n // tn,),
        in_specs=[
            pl.BlockSpec((rows, d), lambda j: (0, 0)),
            pl.BlockSpec((d, tn), lambda j: (0, j)),
            pl.BlockSpec((1, tn), lambda j: (0, j)),
        ],
        out_specs=pl.BlockSpec((rows, tn), lambda j: (0, j)),
        out_shape=jax.ShapeDtypeStruct((rows, n), f32),
        compiler_params=_cparams(1),
        name="adaln",
    )(cv, w, b)


def _head_norm(acc, gain, ones):
    s2 = acc * acc
    hi = s2.astype(bf16)
    lo = (s2 - hi.astype(f32)).astype(bf16)
    ssum = _dot(hi, ones) + _dot(lo, ones)
    return acc * lax.rsqrt(ssum * (1.0 / HEAD_DIM) + RMS_EPS) * gain


def _inproj_kernel(kinds, x_ref, g_ref, sh_ref, sc_ref, w_ref, qg_ref, kg_ref, ones_ref, *rest):
    outs, h_scr = rest[:-1], rest[-1]
    j = pl.program_id(1)

    @pl.when(j == 0)
    def _():
        xf = x_ref[...]
        ms = jnp.mean(xf * xf, axis=-1, keepdims=True)
        y = xf * lax.rsqrt(ms + RMS_EPS) * g_ref[...]
        h_scr[...] = (y * (1.0 + sc_ref[0]) + sh_ref[0]).astype(bf16)

    for jj, (kind, oi) in enumerate(kinds):

        @pl.when(j == jj)
        def _(kind=kind, oi=oi):
            acc = _dot(h_scr[...], w_ref[...])
            o_ref = outs[oi]
            if kind in ("q", "k", "v"):
                for hp in range(N_PAIRS):
                    a = acc[:, hp * LANES:(hp + 1) * LANES]
                    if kind == "q":
                        a = _head_norm(a, qg_ref[:, hp * LANES:(hp + 1) * LANES], ones_ref[...])
                    elif kind == "k":
                        a = _head_norm(a, kg_ref[:, hp * LANES:(hp + 1) * LANES], ones_ref[...])
                    o_ref[hp] = a.astype(bf16)
            elif kind == "g":
                o_ref[...] = _sigmoid(acc).astype(bf16)
            else:
                o_ref[...] = acc.astype(bf16)


def _inproj(x2, gain, shift, scale, w_bf, qg, kg, ones, kinds, col0, tm, tiles_per_batch):
    n, d = x2.shape
    n_out = max(oi for _, oi in kinds) + 1
    out_shapes, out_specs = [], []
    for oi in range(n_out):
        ks = [(jj, k) for jj, (k, o) in enumerate(kinds) if o == oi]
        kind = ks[0][1]
        if kind in ("q", "k", "v"):
            out_shapes.append(jax.ShapeDtypeStruct((N_PAIRS, n, LANES), bf16))
            out_specs.append(pl.BlockSpec((N_PAIRS, tm, LANES), lambda i, j: (0, i, 0)))
        else:
            j0, ncol = ks[0][0], len(ks)
            out_shapes.append(jax.ShapeDtypeStruct((n, d * ncol), bf16))
            out_specs.append(pl.BlockSpec((tm, d), lambda i, j, j0=j0, ncol=ncol: (i, jnp.clip(j - j0, 0, ncol - 1))))
    vec = lambda: pl.BlockSpec((1, d), lambda i, j: (0, 0))
    mod = lambda: pl.BlockSpec((1, 1, d), lambda i, j: (i // tiles_per_batch, 0, 0))
    return pl.pallas_call(
        functools.partial(_inproj_kernel, kinds),
        grid=(n // tm, len(kinds)),
        in_specs=[
            pl.BlockSpec((tm, d), lambda i, j: (i, 0)),
            vec(), mod(), mod(),
            pl.BlockSpec((d, d), lambda i, j: (0, j + col0)),
            vec(), vec(),
            pl.BlockSpec((LANES, LANES), lambda i, j: (0, 0)),
        ],
        out_specs=out_specs,
        out_shape=out_shapes,
        scratch_shapes=[pltpu.VMEM((tm, d), bf16)],
        compiler_params=_cparams(2),
        name="inproj",
    )(x2, gain, shift, scale, w_bf, qg, kg, ones)


def _slab_bias_tables(rpb):
    c = np.arange(GRID_W)
    cs = np.clip(c - WIN_COLS // 2, 0, GRID_W - WIN_COLS)
    j = np.arange(GRID_W)
    col_ok = (j[None, :] >= cs[:, None]) & (j[None, :] < cs[:, None] + WIN_COLS)
    dcol = np.clip(j[None, :] - c[:, None] + WIN_COLS - 1, 0, 2 * WIN_COLS - 2)
    kl = np.arange(SLAB_ROWS)
    spec = {
        0: [(kl + 3, (kl >= 0) & (kl < 8)), (kl + 2, (kl >= 1) & (kl < 9))],
        1: [(kl + 7, kl < 8), (kl + 6, kl < 8)],
        2: [(kl + 5, kl < 8), (kl + 4, kl < 8)],
        3: [(kl + 1, kl >= 2), (kl + 0, kl >= 2)],
        4: [(kl - 1, kl >= 2), (kl - 2, kl >= 2)],
    }
    drow = np.zeros((N_SLAB_TYPES, 2, SLAB_ROWS), np.int32)
    ok = np.zeros((N_SLAB_TYPES, 2, SLAB_ROWS), bool)
    for t, rows in spec.items():
        for qi, (dr, v) in enumerate(rows):
            drow[t, qi] = np.clip(dr, 0, 2 * WIN_ROWS - 2)
            ok[t, qi] = v & (dr >= 0) & (dr <= 2 * WIN_ROWS - 2)
    vals = rpb[:, drow[:, :, None, :, None], dcol[None, None, :, None, :]].astype(f32)
    valid = ok[:, :, None, :, None] & col_ok[None, None, :, None, :]
    vals = jnp.where(valid[None], vals, NEG_BIG)
    return vals.reshape(N_HEADS, N_SLAB_TYPES, 2 * GRID_W, SLAB_ROWS * GRID_W)


def _attn_kernel(n_rows, q_ref, k_ref, v_ref, kc_ref, vc_ref, sb_ref, o_ref):
    rb = pl.program_id(2)
    base = jnp.clip(QROWS * rb - WIN_ROWS // 2, 0, n_rows - KROWS)
    lane = lax.broadcasted_iota(i32, (2 * GRID_W, LANES), 1)
    kc = kc_ref[0]
    vc = vc_ref[0]
    slab = SLAB_ROWS * GRID_W
    for ip in range(QROWS // 2):
        r0 = QROWS * rb + 2 * ip
        rs0 = jnp.clip(r0 - WIN_ROWS // 2, 0, n_rows - WIN_ROWS)
        s = jnp.minimum(rs0 - base, KROWS - SLAB_ROWS)
        typ = jnp.where(r0 < 2, 1, jnp.where(r0 < 4, 2, jnp.where(r0 == n_rows - 4, 3, jnp.where(r0 == n_rows - 2, 4, 0))))
        off = pl.multiple_of(s * GRID_W, GRID_W)
        kw = k_ref[pl.ds(off, slab), :]
        vw = v_ref[pl.ds(off, slab), :]
        qp = q_ref[0, ip * 2 * GRID_W:(ip + 1) * 2 * GRID_W, :]
        res = []
        for hh in range(2):
            head_lanes = (lane < HEAD_DIM) if hh == 0 else (lane >= HEAD_DIM)
            qm = jnp.where(head_lanes, qp, jnp.zeros_like(qp))
            sw = _dot_nt(qm, kw) + sb_ref[hh, typ]
            sc = _dot_nt(qm, kc)
            m = jnp.maximum(jnp.max(sw, axis=-1, keepdims=True), jnp.max(sc, axis=-1, keepdims=True))
            pw = jnp.exp(sw - m)
            pc = jnp.exp(sc - m)
            den = jnp.sum(pw, axis=-1, keepdims=True) + jnp.sum(pc, axis=-1, keepdims=True)
            o = _dot(pw.astype(bf16), vw) + _dot(pc.astype(bf16), vc)
            res.append(o / den)
        o_ref[0, ip * 2 * GRID_W:(ip + 1) * 2 * GRID_W, :] = jnp.where(lane < HEAD_DIM, res[0], res[1]).astype(bf16)


def _attention(q3, k3, v3, kc3, vc3, sb, batch, seq):
    n_rows = seq // GRID_W
    assert n_rows % QROWS == 0 and n_rows >= KROWS
    nrb = n_rows // QROWS
    tq = QROWS * GRID_W
    tk = KROWS * GRID_W

    n_tok = batch * seq

    def kv_map(b, hp, rb):
        base = jnp.clip(QROWS * rb - WIN_ROWS // 2, 0, n_rows - KROWS)
        return ((hp * (n_tok // GRID_W) + b * n_rows + base) * GRID_W, 0)

    kv_spec = lambda: pl.BlockSpec((pl.Element(tk), pl.Element(LANES)), kv_map)
    return pl.pallas_call(
        functools.partial(_attn_kernel, n_rows),
        grid=(batch, N_PAIRS, nrb),
        in_specs=[
            pl.BlockSpec((1, tq, LANES), lambda b, hp, rb: (hp, b * nrb + rb, 0)),
            kv_spec(), kv_spec(),
            pl.BlockSpec((1, CTX_LEN, LANES), lambda b, hp, rb: (hp, b, 0)),
            pl.BlockSpec((1, CTX_LEN, LANES), lambda b, hp, rb: (hp, b, 0)),
            pl.BlockSpec((2, N_SLAB_TYPES, 2 * GRID_W, SLAB_ROWS * GRID_W), lambda b, hp, rb: (hp, 0, 0, 0)),
        ],
        out_specs=pl.BlockSpec((1, tq, LANES), lambda b, hp, rb: (hp, b * nrb + rb, 0)),
        out_shape=jax.ShapeDtypeStruct(q3.shape, bf16),
        compiler_params=_cparams(3),
        name="attn",
    )(q3, k3.reshape(N_PAIRS * n_tok, LANES), v3.reshape(N_PAIRS * n_tok, LANES), kc3, vc3, sb)


A_CHUNK = 8


def _filter_kernel(seq, w1t_ref, w1c_ref, w1s_ref, b1_ref, fr1_ref, w2t_ref, b2_ref, fr2_ref, w3t_ref, dl_ref,
                   fq_ref, o_ref):
    st = pl.program_id(0)
    npos = A_CHUNK * LANES
    n = st * npos + lax.broadcasted_iota(i32, (1, npos), 1)
    idx = jnp.where(n < seq, n, 2 * seq - n).astype(f32)
    t = idx / (seq - 1)
    ang = (2.0 * math.pi / seq) * idx
    fa = fq_ref[...] * ang
    z1 = w1t_ref[...] * t + _dot(w1c_ref[...], jnp.cos(fa), HI) + _dot(w1s_ref[...], -jnp.sin(fa), HI) + b1_ref[...]
    h1 = jnp.sin(fr1_ref[...] * z1)
    h2 = jnp.sin(fr2_ref[...] * (_dot(w2t_ref[...], h1, HI) + b2_ref[...]))
    filt = _dot(w3t_ref[0], h2, HI) * jnp.exp(-t * dl_ref[...])
    filt = jnp.where(n == seq, 0.0, filt)
    for al in range(A_CHUNK):
        o_ref[:, al, :] = filt[:, al * LANES:(al + 1) * LANES]


def _hyena_filter(seq, w1, b1, fr1, w2, b2, fr2, w3):
    na = 2 * seq // LANES
    steps = na // A_CHUNK
    fwd_steps = steps // 2
    col = lambda v: v.reshape(-1, 1).astype(f32)
    deltas = jnp.abs(jnp.linspace(math.log(DECAY_TARGET) / SLOW_DECAY_PCT, math.log(DECAY_TARGET) / FAST_DECAY_PCT,
                                  D_B, dtype=f32))
    freqs = jnp.linspace(1e-4, N_BANDS - 1, N_BANDS, dtype=f32)
    w3t = w3.astype(f32).T.reshape(2, D_B, FILT_HID)
    w1 = w1.astype(f32)
    args = (w1[0:1].T, w1[1:1 + N_BANDS].T, w1[1 + N_BANDS:].T, col(b1), col(fr1), w2.astype(f32).T, col(b2), col(fr2),
            w3t, col(deltas), col(freqs))
    full = lambda a: pl.BlockSpec(a.shape, lambda s: (0,) * a.ndim)
    in_specs = [full(a) for a in args]
    in_specs[8] = pl.BlockSpec((1, D_B, FILT_HID), lambda s: (s // fwd_steps, 0, 0))
    return pl.pallas_call(
        functools.partial(_filter_kernel, seq),
        grid=(steps,),
        in_specs=in_specs,
        out_specs=pl.BlockSpec((D_B, A_CHUNK, LANES), lambda s: (0, s, 0)),
        out_shape=jax.ShapeDtypeStruct((D_B, na, LANES), f32),
        compiler_params=_cparams(1),
        name="filt",
    )(*args)


def _dft_consts(seq):
    n = 2 * seq
    n1 = n // LANES
    a_used = seq // LANES
    k1 = np.arange(n1)[:, None]
    a = np.arange(n1)[None, :]
    ang1 = -2.0 * np.pi * ((k1 * a) % n1) / n1
    f1 = np.concatenate([np.cos(ang1), np.sin(ang1)], axis=0)
    b = np.arange(LANES)[None, :]
    angt = -2.0 * np.pi * ((k1 * b) % n) / n
    tw_r, tw_i = np.cos(angt), np.sin(angt)
    bb = np.arange(LANES)[:, None]
    k2 = np.arange(LANES)[None, :]
    angg = -2.0 * np.pi * ((bb * k2) % LANES) / LANES
    gr, gi = np.cos(angg), np.sin(angg)
    g_fwd = np.block([[gr, gi], [-gi, gr]])
    g_inv = np.block([[gr, -gi], [gi, gr]])
    f1_inv = np.concatenate([np.cos(ang1).T[:a_used], np.sin(ang1).T[:a_used]], axis=1)
    c = lambda x: jnp.asarray(x, f32)
    return dict(n1=n1, a_used=a_used, f1_full=c(f1), f1=c(f1[:, :a_used]), tw_r=c(tw_r), tw_i=c(tw_i), g_fwd=c(g_fwd),
                g_inv=c(g_inv), f1_inv=c(f1_inv))


def _fft_fwd(x, f1, tw_r, tw_i, g_fwd, precision, cast):
    db = x.shape[0]
    n1 = tw_r.shape[0]
    f1b = jnp.broadcast_to(cast(f1)[None], (db,) + f1.shape)
    y = _bdot(f1b, cast(x), precision)
    yr, yi = y[:, :n1], y[:, n1:]
    ytr = yr * tw_r - yi * tw_i
    yti = yr * tw_i + yi * tw_r
    lhs = jnp.concatenate([ytr, yti], axis=-1).reshape(db * n1, 2 * LANES)
    return _dot(cast(lhs), cast(g_fwd), precision)


def _fftk_kernel(x_ref, f1_ref, twr_ref, twi_ref, g_ref, o_ref):
    db, n1 = x_ref.shape[0], twr_ref.shape[0]
    z = _fft_fwd(x_ref[...], f1_ref[...], twr_ref[...], twi_ref[...], g_ref[...], HI, lambda v: v)
    o_ref[...] = z.reshape(db, n1, 2 * LANES)


def _filter_spectrum(kf, dc):
    d, n1, _ = kf.shape
    db = 16
    full = lambda a: pl.BlockSpec(a.shape, lambda i: (0,) * a.ndim)
    consts = (dc["f1_full"], dc["tw_r"], dc["tw_i"], dc["g_fwd"])
    return pl.pallas_call(
        _fftk_kernel,
        grid=(d // db,),
        in_specs=[pl.BlockSpec((db, n1, LANES), lambda i: (i, 0, 0))] + [full(a) for a in consts],
        out_specs=pl.BlockSpec((db, n1, 2 * LANES), lambda i: (i, 0, 0)),
        out_shape=jax.ShapeDtypeStruct((d, n1, 2 * LANES), f32),
        compiler_params=_cparams(1),
        name="fftk",
    )(kf, *consts)


def _hpre_kernel(n_chunks, u_ref, up_ref, un_ref, cw_ref, cb_ref, zt_ref, x0_ref):
    ac = pl.program_id(1)
    tm = u_ref.shape[0]
    has_prev = (ac > 0).astype(f32)
    has_next = (ac < n_chunks - 1).astype(f32)
    row = lax.broadcasted_iota(i32, (tm, LANES), 0)
    halo = up_ref.shape[0]

    def conv(c0):
        sl = slice(c0, c0 + LANES)
        u = u_ref[:, sl].astype(f32)
        prev = up_ref[halo - 1:halo, sl].astype(f32) * has_prev
        nxt = un_ref[0:1, sl].astype(f32) * has_next
        um = jnp.where(row == 0, prev, pltpu.roll(u, 1, 0))
        upl = jnp.where(row == tm - 1, nxt, pltpu.roll(u, tm - 1, 0))
        return um * cw_ref[0:1, sl] + u * cw_ref[1:2, sl] + upl * cw_ref[2:3, sl] + cb_ref[:, sl]

    for dt in range(D_B // LANES):
        v = conv(dt * LANES)
        x1 = conv(D_B + dt * LANES)
        x0_ref[:, dt * LANES:(dt + 1) * LANES] = conv(2 * D_B + dt * LANES).astype(bf16)
        z = v * x1
        for al in range(A_CHUNK):
            zt_ref[0, dt * LANES:(dt + 1) * LANES, al, :] = z[al * LANES:(al + 1) * LANES, :].T


def _hyena_pre(u, conv_w, conv_b, batch, seq):
    n = u.shape[0]
    tm = A_CHUNK * LANES
    n_chunks = seq // tm
    halo = 16
    hb = tm // halo
    cw = jnp.concatenate([conv_w.astype(f32), jnp.zeros((8 - conv_w.shape[0], conv_w.shape[1]), f32)], axis=0)
    return pl.pallas_call(
        functools.partial(_hpre_kernel, n_chunks),
        grid=(batch, n_chunks),
        in_specs=[
            pl.BlockSpec((tm, 3 * D_B), lambda b, c: (b * n_chunks + c, 0)),
            pl.BlockSpec((halo, 3 * D_B), lambda b, c: (jnp.maximum((b * n_chunks + c) * hb - 1, 0), 0)),
            pl.BlockSpec((halo, 3 * D_B), lambda b, c: (jnp.minimum((b * n_chunks + c + 1) * hb, n // halo - 1), 0)),
            pl.BlockSpec((8, 3 * D_B), lambda b, c: (0, 0)),
            pl.BlockSpec((1, 3 * D_B), lambda b, c: (0, 0)),
        ],
        out_specs=[
            pl.BlockSpec((1, D_B, A_CHUNK, LANES), lambda b, c: (b, 0, c, 0)),
            pl.BlockSpec((tm, D_B), lambda b, c: (b * n_chunks + c, 0)),
        ],
        out_shape=[
            jax.ShapeDtypeStruct((batch, D_B, seq // LANES, LANES), f32),
            jax.ShapeDtypeStruct((n, D_B), bf16),
        ],
        compiler_params=_cparams(2),
        name="hpre",
    )(u, u, u, cw, conv_b.reshape(1, -1).astype(f32))


def _fftconv_kernel(z_ref, kh_ref, skip_ref, f1_ref, twr_ref, twi_ref, gf_ref, gi_ref, f1i_ref, y_ref):
    db, a_used = z_ref.shape[1], z_ref.shape[2]
    n1 = twr_ref.shape[0]
    tw_r, tw_i = twr_ref[...], twi_ref[...]
    cast = lambda v: v.astype(bf16)
    z = z_ref[0]
    spec = _fft_fwd(z, f1_ref[...], tw_r, tw_i, gf_ref[...], None, cast)
    kh = kh_ref[...].reshape(db * n1, 2 * LANES)
    sr, si = spec[:, :LANES], spec[:, LANES:]
    kr, ki = kh[:, :LANES], kh[:, LANES:]
    prod = jnp.concatenate([sr * kr - si * ki, sr * ki + si * kr], axis=-1)
    q = _dot(cast(prod), cast(gi_ref[...])).reshape(db, n1, 2 * LANES)
    qr, qi = q[:, :, :LANES], q[:, :, LANES:]
    rhs = jnp.concatenate([qr * tw_r + qi * tw_i, qi * tw_r - qr * tw_i], axis=1)
    f1ib = jnp.broadcast_to(cast(f1i_ref[...])[None], (db, a_used, 2 * n1))
    y = _bdot(f1ib, cast(rhs)) * (1.0 / (n1 * LANES))
    y_ref[0] = y + z * skip_ref[...]


def _fftconv(zt, khat, skip, dc):
    batch, d, a_used, _ = zt.shape
    n1 = dc["n1"]
    db = 16
    consts = (dc["f1"], dc["tw_r"], dc["tw_i"], dc["g_fwd"], dc["g_inv"], dc["f1_inv"])
    full = lambda a: pl.BlockSpec(a.shape, lambda i, b: (0,) * a.ndim)
    return pl.pallas_call(
        _fftconv_kernel,
        grid=(d // db, batch),
        in_specs=[
            pl.BlockSpec((1, db, a_used, LANES), lambda i, b: (b, i, 0, 0)),
            pl.BlockSpec((db, n1, 2 * LANES), lambda i, b: (i, 0, 0)),
            pl.BlockSpec((db, 1, 1), lambda i, b: (i, 0, 0)),
        ] + [full(a) for a in consts],
        out_specs=pl.BlockSpec((1, db, a_used, LANES), lambda i, b: (b, i, 0, 0)),
        out_shape=jax.ShapeDtypeStruct(zt.shape, f32),
        compiler_params=_cparams(2),
        name="fftconv",
    )(zt, khat, skip.reshape(d, 1, 1).astype(f32), *consts)


def _hpost_kernel(yt_ref, x0_ref, ob_ref):
    for dt in range(D_B // LANES):
        for al in range(A_CHUNK):
            rows = slice(al * LANES, (al + 1) * LANES)
            cols = slice(dt * LANES, (dt + 1) * LANES)
            y = yt_ref[0, cols, al, :].T
            ob_ref[rows, cols] = (x0_ref[rows, cols].astype(f32) * y).astype(bf16)


def _hyena_post(yt, x0, batch, seq):
    tm = A_CHUNK * LANES
    n_chunks = seq // tm
    return pl.pallas_call(
        _hpost_kernel,
        grid=(batch, n_chunks),
        in_specs=[
            pl.BlockSpec((1, D_B, A_CHUNK, LANES), lambda b, c: (b, 0, c, 0)),
            pl.BlockSpec((tm, D_B), lambda b, c: (b * n_chunks + c, 0)),
        ],
        out_specs=pl.BlockSpec((tm, D_B), lambda b, c: (b * n_chunks + c, 0)),
        out_shape=jax.ShapeDtypeStruct(x0.shape, bf16),
        compiler_params=_cparams(2),
        name="hpost",
    )(yt, x0)


IDX_ROWS = 8


def _merge_kernel(oa_ref, ob_ref, ga_ref, gb_ref, x_ref, g1_ref, wba_ref, wbb_ref, wout_ref, n2g_ref, sh2_ref, sc2_ref,
                  wrt_ref, br_ref, x1_ref, h2_ref, idx_ref, wt_ref):
    oa = jnp.concatenate([oa_ref[hp] for hp in range(N_PAIRS)], axis=1)
    m = ga_ref[...].astype(f32) * _dot(oa, wba_ref[...]) + gb_ref[...].astype(f32) * _dot(ob_ref[...], wbb_ref[...])
    x1 = x_ref[...] + g1_ref[0] * _dot(m.astype(bf16), wout_ref[...])
    x1_ref[...] = x1
    ms = jnp.mean(x1 * x1, axis=-1, keepdims=True)
    h2 = (x1 * lax.rsqrt(ms + RMS_EPS) * n2g_ref[...]) * (1.0 + sc2_ref[0]) + sh2_ref[0]
    h2_ref[...] = h2.astype(bf16)
    logits = _dot_nt(wrt_ref[...], h2, HI) + br_ref[...]
    eio = lax.broadcasted_iota(i32, logits.shape, 0)
    vals = logits
    idxs, tops = [], []
    for _ in range(TOP_K):
        mx = jnp.max(vals, axis=0, keepdims=True)
        ix = jnp.min(jnp.where(vals == mx, eio, N_EXPERTS), axis=0, keepdims=True)
        idxs.append(ix)
        tops.append(mx)
        vals = jnp.where(eio == ix, -jnp.inf, vals)
    ex = [jnp.exp(v - tops[0]) for v in tops]
    den = ex[0] + ex[1] + ex[2] + ex[3]
    idx_ref[...] = jnp.concatenate(idxs + [jnp.full_like(idxs[0], -1)] * (IDX_ROWS - TOP_K), axis=0)
    wt_ref[...] = jnp.concatenate([e / den for e in ex] + [jnp.zeros_like(den)] * (IDX_ROWS - TOP_K), axis=0)


def _merge(oa3, ob, ga, gb, x2, g1, wba, wbb, wout, n2g, sh2, sc2, wrt, br, tm, tiles_per_batch):
    n, d = x2.shape
    tok = lambda: pl.BlockSpec((tm, d), lambda i: (i, 0))
    mod = lambda: pl.BlockSpec((1, 1, d), lambda i: (i // tiles_per_batch, 0, 0))
    wsp = lambda: pl.BlockSpec((d, d), lambda i: (0, 0))
    return pl.pallas_call(
        _merge_kernel,
        grid=(n // tm,),
        in_specs=[
            pl.BlockSpec((N_PAIRS, tm, LANES), lambda i: (0, i, 0)),
            tok(), tok(), tok(), tok(), mod(), wsp(), wsp(), wsp(),
            pl.BlockSpec((1, d), lambda i: (0, 0)), mod(), mod(),
            pl.BlockSpec((N_EXPERTS, d), lambda i: (0, 0)),
            pl.BlockSpec((N_EXPERTS, 1), lambda i: (0, 0)),
        ],
        out_specs=[tok(), tok(), pl.BlockSpec((IDX_ROWS, tm), lambda i: (0, i)), pl.BlockSpec((IDX_ROWS, tm), lambda i: (0, i))],
        out_shape=[
            jax.ShapeDtypeStruct((n, d), f32),
            jax.ShapeDtypeStruct((n, d), bf16),
            jax.ShapeDtypeStruct((IDX_ROWS, n), i32),
            jax.ShapeDtypeStruct((IDX_ROWS, n), f32),
        ],
        compiler_params=_cparams(1),
        name="merge",
    )(oa3, ob, ga, gb, x2, g1, wba, wbb, wout, n2g, sh2, sc2, wrt, br)


MOE_TILE = 1024
MOE_ROWS = 128


def _moe_kernel(h2_ref, idx_ref, wt_ref, wgu_ref, bgu_ref, wd_ref, bd_ref, y_ref, rank_scr, msk_scr, wte_scr):
    e = pl.program_id(1)
    t = h2_ref.shape[0]

    @pl.when(e == 0)
    def _():
        y_ref[...] = jnp.zeros_like(y_ref)
        idx = idx_ref[...]
        w = wt_ref[...]
        eio = lax.broadcasted_iota(i32, (N_EXPERTS, t), 0)
        msk = jnp.zeros((N_EXPERTS, t), f32)
        wte = jnp.zeros((N_EXPERTS, t), f32)
        for k in range(TOP_K):
            hit = idx[k:k + 1, :] == eio
            msk = msk + hit.astype(f32)
            wte = wte + jnp.where(hit, w[k:k + 1, :], 0.0)
        lane = lax.broadcasted_iota(i32, (N_EXPERTS, t), 1)
        csum = msk
        s = 1
        while s < t:
            csum = csum + jnp.where(lane >= s, pltpu.roll(csum, s, 1), 0.0)
            s *= 2
        rank_scr[...] = (csum - msk).astype(i32)
        msk_scr[...] = msk
        wte_scr[...] = wte

    rank = rank_scr[pl.ds(e, 1), :]
    sel = msk_scr[pl.ds(e, 1), :] > 0.0
    wte = wte_scr[pl.ds(e, 1), :]
    count = jnp.sum(msk_scr[pl.ds(e, 1), :]).astype(i32)
    n_chunks = (count + MOE_ROWS - 1) // MOE_ROWS

    def chunk(jc, carry):
        jio = lax.broadcasted_iota(i32, (MOE_ROWS, t), 0)
        hit = ((rank - jc * MOE_ROWS) == jio) & sel
        onehot = hit.astype(bf16)
        xg = _dot(onehot, h2_ref[...]).astype(bf16)
        gu = _dot(xg, wgu_ref[0]) + bgu_ref[0]
        gate = jnp.minimum(gu[:, :D_FF], SWIGLU_LIMIT)
        up = jnp.clip(gu[:, D_FF:], -SWIGLU_LIMIT, SWIGLU_LIMIT)
        act = (up + 1.0) * gate * _sigmoid(SWIGLU_ALPHA * gate)
        o = _dot(act.astype(bf16), wd_ref[0]) + bd_ref[0]
        wrow = jnp.sum(jnp.where(hit, wte, 0.0), axis=1, keepdims=True)
        y_ref[...] += _dot_tn(onehot, (o * wrow).astype(bf16))
        return carry

    lax.fori_loop(0, n_chunks, chunk, 0)


def _moe(h2, idx, wt, wgu, bgu, wd, bd):
    n, d = h2.shape
    t = MOE_TILE
    return pl.pallas_call(
        _moe_kernel,
        grid=(n // t, N_EXPERTS),
        in_specs=[
            pl.BlockSpec((t, d), lambda i, e: (i, 0)),
            pl.BlockSpec((IDX_ROWS, t), lambda i, e: (0, i)),
            pl.BlockSpec((IDX_ROWS, t), lambda i, e: (0, i)),
            pl.BlockSpec((1, d, 2 * D_FF), lambda i, e: (e, 0, 0)),
            pl.BlockSpec((1, 1, 2 * D_FF), lambda i, e: (e, 0, 0)),
            pl.BlockSpec((1, D_FF, d), lambda i, e: (e, 0, 0)),
            pl.BlockSpec((1, 1, d), lambda i, e: (e, 0, 0)),
        ],
        out_specs=pl.BlockSpec((t, d), lambda i, e: (i, 0)),
        out_shape=jax.ShapeDtypeStruct((n, d), f32),
        scratch_shapes=[pltpu.VMEM((N_EXPERTS, t), i32), pltpu.VMEM((N_EXPERTS, t), f32), pltpu.VMEM((N_EXPERTS, t), f32)],
        compiler_params=_cparams(2),
        name="moe",
    )(h2, idx, wt, wgu, bgu, wd, bd)


def _final_kernel(x1_ref, y_ref, g2_ref, o_ref):
    o_ref[...] = x1_ref[...] + g2_ref[0] * y_ref[...]


def _final(x1, y, g2, tm, tiles_per_batch):
    n, d = x1.shape
    tok = lambda: pl.BlockSpec((tm, d), lambda i: (i, 0))
    return pl.pallas_call(
        _final_kernel,
        grid=(n // tm,),
        in_specs=[tok(), tok(), pl.BlockSpec((1, 1, d), lambda i: (i // tiles_per_batch, 0, 0))],
        out_specs=tok(),
        out_shape=jax.ShapeDtypeStruct((n, d), f32),
        compiler_params=_cparams(1),
        name="final",
    )(x1, y, g2)


def _layer(x, c, ctx, c_ctx, w_ada, b_ada, norm1_g, norm2_g, w_in, q_norm_g, k_norm_g, rpb, conv_w, conv_b, filt_w1,
           filt_b1, filt_freq1, filt_w2, filt_b2, filt_freq2, filt_w3, hyena_skip, w_branch_a, w_branch_b, w_out,
           w_router, b_router, w_gate_up, b_gate_up, w_down, b_down):
    batch, seq, d = x.shape
    n = batch * seq
    n_ctx = ctx.shape[1]
    assert d == D_MODEL and n_ctx == CTX_LEN and seq % (A_CHUNK * LANES) == 0 and n % MOE_TILE == 0

    rows = 8 * ((batch + 1 + 7) // 8)
    cv = jnp.zeros((rows, d), f32).at[:batch].set(c).at[batch].set(c_ctx)
    mod = _adaln(cv, w_ada, b_ada.reshape(1, -1))
    sh1, sc1, g1, sh2, sc2, g2 = [mod[:batch, i * d:(i + 1) * d].reshape(batch, 1, d) for i in range(N_MOD)]
    modc = jnp.broadcast_to(mod[batch:batch + 1], (batch, N_MOD * d))
    sh1c, sc1c = modc[:, :d].reshape(batch, 1, d), modc[:, d:2 * d].reshape(batch, 1, d)

    w_in_bf = w_in.astype(bf16)
    ones = jnp.asarray(np.kron(np.eye(2), np.ones((HEAD_DIM, HEAD_DIM))), bf16)
    qg = (jnp.tile(q_norm_g.astype(f32), N_HEADS) * (HEAD_DIM ** -0.5)).reshape(1, d)
    kg = jnp.tile(k_norm_g.astype(f32), N_HEADS).reshape(1, d)
    n1g = norm1_g.reshape(1, d)

    kc3, vc3 = _inproj(ctx.reshape(batch * n_ctx, d), n1g, sh1c, sc1c, w_in_bf, qg, kg, ones,
                       (("k", 0), ("v", 1)), 1, n_ctx, 1)
    tm = 512
    x2 = x.reshape(n, d)
    kinds = (("q", 0), ("k", 1), ("v", 2), ("u", 3), ("u", 3), ("u", 3), ("g", 4), ("g", 5))
    q3, k3, v3, u, ga, gb = _inproj(x2, n1g, sh1, sc1, w_in_bf, qg, kg, ones, kinds, 0, tm, seq // tm)

    oa3 = _attention(q3, k3, v3, kc3, vc3, _slab_bias_tables(rpb), batch, seq)

    dc = _dft_consts(seq)
    kf = _hyena_filter(seq, filt_w1, filt_b1, filt_freq1, filt_w2, filt_b2, filt_freq2, filt_w3)
    khat = _filter_spectrum(kf, dc)
    zt, x0 = _hyena_pre(u, conv_w, conv_b, batch, seq)
    yt = _fftconv(zt, khat, hyena_skip, dc)
    ob = _hyena_post(yt, x0, batch, seq)

    x1, h2, idx, wt = _merge(oa3, ob, ga, gb, x2, g1, w_branch_a.astype(bf16), w_branch_b.astype(bf16),
                             w_out.astype(bf16), norm2_g.reshape(1, d), sh2, sc2, w_router.astype(f32).T,
                             b_router.reshape(-1, 1).astype(f32), tm, seq // tm)
    y = _moe(h2, idx, wt, w_gate_up.astype(bf16), b_gate_up.reshape(N_EXPERTS, 1, -1).astype(f32),
             w_down.astype(bf16), b_down.reshape(N_EXPERTS, 1, -1).astype(f32))
    out = _final(x1, y, g2, tm, seq // tm)
    return out.reshape(batch, seq, d)


def kernel(x, c, ctx, c_ctx, w_ada, b_ada, norm1_g, norm2_g, w_in, q_norm_g, k_norm_g, rpb, conv_w, conv_b, filt_w1, filt_b1, filt_freq1, filt_w2, filt_b2, filt_freq2, filt_w3, hyena_skip, w_branch_a, w_branch_b, w_out, w_router, b_router, w_gate_up, b_gate_up, w_down, b_down):
    assert w_ada.shape[0] == 1, "single-layer stack"
    return _layer(x, c, ctx, c_ctx, w_ada[0], b_ada[0], norm1_g[0], norm2_g[0], w_in[0], q_norm_g[0], k_norm_g[0],
                  rpb[0], conv_w[0], conv_b[0], filt_w1[0], filt_b1[0], filt_freq1[0], filt_w2[0], filt_b2[0],
                  filt_freq2[0], filt_w3[0], hyena_skip[0], w_branch_a[0], w_branch_b[0], w_out[0], w_router[0],
                  b_router[0], w_gate_up[0], b_gate_up[0], w_down[0], b_down[0])
```

```python
import functools
import math

import numpy as np
import jax
import jax.numpy as jnp
from jax import lax
from jax.experimental import pallas as pl
from jax.experimental.pallas import tpu as pltpu

f32, bf16, i32 = jnp.float32, jnp.bfloat16, jnp.int32
HI = lax.Precision.HIGHEST

LANES = 128
D_MODEL = 1024
N_HEADS = 16
HEAD_DIM = 64
N_PAIRS = N_HEADS // 2
GRID_W = 64
WIN_ROWS = 8
WIN_COLS = 16
CTX_LEN = 256
N_MOD = 6
RMS_EPS = 1e-6
D_B = 1024
EMB_DIM = 33
FILT_HID = 64
N_BANDS = (EMB_DIM - 1) // 2
DECAY_TARGET = 1e-2
FAST_DECAY_PCT = 0.3
SLOW_DECAY_PCT = 1.5
N_EXPERTS = 32
TOP_K = 4
D_FF = 1024
SWIGLU_LIMIT = 7.0
SWIGLU_ALPHA = 1.702
NEG_BIG = -1e30

QROWS = 8
KROWS = 16
SLAB_ROWS = 10
N_SLAB_TYPES = 5
VMEM_LIMIT = 56 * 1024 * 1024


def _cparams(ndim, vmem=VMEM_LIMIT):
    return pltpu.CompilerParams(dimension_semantics=("arbitrary",) * ndim, vmem_limit_bytes=vmem)


def _sigmoid(x):
    return 1.0 / (1.0 + jnp.exp(-x))


def _dot(a, b, precision=None):
    return jnp.dot(a, b, precision=precision, preferred_element_type=f32)


def _dot_nt(a, b, precision=None):
    return lax.dot_general(a, b, (((1,), (1,)), ((), ())), precision=precision, preferred_element_type=f32)


def _dot_tn(a, b):
    return lax.dot_general(a, b, (((0,), (0,)), ((), ())), preferred_element_type=f32)


def _bdot(a, b, precision=None):
    return lax.dot_general(a, b, (((2,), (1,)), ((0,), (0,))), precision=precision, preferred_element_type=f32)


def _adaln_kernel(cv_ref, w_ref, b_ref, o_ref):
    cv = cv_ref[...]
    o_ref[...] = _dot(cv * _sigmoid(cv), w_ref[...], HI) + b_ref[...]


def _adaln(cv, w, b):
    rows, d = cv.shape
    n = w.shape[1]
    tn = 1024
    return pl.pallas_call(
        _adaln_kernel,
        grid=(n // tn,),
        in_specs=[
            pl.BlockSpec((rows, d), lambda j: (0, 0)),
            pl.BlockSpec((d, tn), lambda j: (0, j)),
            pl.BlockSpec((1, tn), lambda j: (0, j)),
        ],
        out_specs=pl.BlockSpec((rows, tn), lambda j: (0, j)),
        out_shape=jax.ShapeDtypeStruct((rows, n), f32),
        compiler_params=_cparams(1),
        name="adaln",
    )(cv, w, b)


def _head_norm(acc, gain, ones):
    s2 = acc * acc
    hi = s2.astype(bf16)
    lo = (s2 - hi.astype(f32)).astype(bf16)
    ssum = _dot(hi, ones) + _dot(lo, ones)
    return acc * lax.rsqrt(ssum * (1.0 / HEAD_DIM) + RMS_EPS) * gain


def _inproj_kernel(kinds, x_ref, g_ref, sh_ref, sc_ref, w_ref, qg_ref, kg_ref, ones_ref, *rest):
    outs, h_scr = rest[:-1], rest[-1]
    j = pl.program_id(1)

    @pl.when(j == 0)
    def _():
        xf = x_ref[...]
        ms = jnp.mean(xf * xf, axis=-1, keepdims=True)
        y = xf * lax.rsqrt(ms + RMS_EPS) * g_ref[...]
        h_scr[...] = (y * (1.0 + sc_ref[0]) + sh_ref[0]).astype(bf16)

    for jj, (kind, oi) in enumerate(kinds):

        @pl.when(j == jj)
        def _(kind=kind, oi=oi):
            acc = _dot(h_scr[...], w_ref[...])
            o_ref = outs[oi]
            if kind in ("q", "k", "v"):
                for hp in range(N_PAIRS):
                    a = acc[:, hp * LANES:(hp + 1) * LANES]
                    if kind == "q":
                        a = _head_norm(a, qg_ref[:, hp * LANES:(hp + 1) * LANES], ones_ref[...])
                    elif kind == "k":
                        a = _head_norm(a, kg_ref[:, hp * LANES:(hp + 1) * LANES], ones_ref[...])
                    o_ref[hp] = a.astype(bf16)
            elif kind == "g":
                o_ref[...] = _sigmoid(acc).astype(bf16)
            else:
                o_ref[...] = acc.astype(bf16)


def _inproj(x2, gain, shift, scale, w_bf, qg, kg, ones, kinds, col0, tm, tiles_per_batch):
    n, d = x2.shape
    n_out = max(oi for _, oi in kinds) + 1
    out_shapes, out_specs = [], []
    for oi in range(n_out):
        ks = [(jj, k) for jj, (k, o) in enumerate(kinds) if o == oi]
        kind = ks[0][1]
        if kind in ("q", "k", "v"):
            out_shapes.append(jax.ShapeDtypeStruct((N_PAIRS, n, LANES), bf16))
            out_specs.append(pl.BlockSpec((N_PAIRS, tm, LANES), lambda i, j: (0, i, 0)))
        else:
            j0, ncol = ks[0][0], len(ks)
            out_shapes.append(jax.ShapeDtypeStruct((n, d * ncol), bf16))
            out_specs.append(pl.BlockSpec((tm, d), lambda i, j, j0=j0, ncol=ncol: (i, jnp.clip(j - j0, 0, ncol - 1))))
    vec = lambda: pl.BlockSpec((1, d), lambda i, j: (0, 0))
    mod = lambda: pl.BlockSpec((1, 1, d), lambda i, j: (i // tiles_per_batch, 0, 0))
    return pl.pallas_call(
        functools.partial(_inproj_kernel, kinds),
        grid=(n // tm, len(kinds)),
        in_specs=[
            pl.BlockSpec((tm, d), lambda i, j: (i, 0)),
            vec(), mod(), mod(),
            pl.BlockSpec((d, d), lambda i, j: (0, j + col0)),
            vec(), vec(),
            pl.BlockSpec((LANES, LANES), lambda i, j: (0, 0)),
        ],
        out_specs=out_specs,
        out_shape=out_shapes,
        scratch_shapes=[pltpu.VMEM((tm, d), bf16)],
        compiler_params=_cparams(2),
        name="inproj",
    )(x2, gain, shift, scale, w_bf, qg, kg, ones)


def _slab_bias_tables(rpb):
    n_drow = 2 * WIN_ROWS - 1
    c = np.arange(GRID_W)
    cs = np.clip(c - WIN_COLS // 2, 0, GRID_W - WIN_COLS)
    j = np.arange(GRID_W)
    col_ok = (j[None, :] >= cs[:, None]) & (j[None, :] < cs[:, None] + WIN_COLS)
    pad = GRID_W - WIN_COLS
    rp = jnp.pad(rpb.astype(f32), ((0, 0), (0, 0), (pad, pad)))
    bm = jnp.stack([rp[:, :, pad + WIN_COLS - 1 - ci:pad + WIN_COLS - 1 - ci + GRID_W] for ci in range(GRID_W)], axis=2)
    bm = jnp.where(col_ok[None, None], bm, NEG_BIG)
    spec = {0: [(3, 0, 8), (2, 1, 9)], 1: [(7, 0, 8), (6, 0, 8)], 2: [(5, 0, 8), (4, 0, 8)],
            3: [(1, 2, 10), (0, 2, 10)], 4: [(-1, 2, 10), (-2, 2, 10)]}
    dpad = 2
    bmp = jnp.pad(bm, ((0, 0), (dpad, dpad + 1), (0, 0), (0, 0)), constant_values=NEG_BIG)
    kl = np.arange(SLAB_ROWS)
    slabs = []
    for t in range(N_SLAB_TYPES):
        for start, lo, hi in spec[t]:
            assert start + dpad >= 0 and start + dpad + SLAB_ROWS <= n_drow + 2 * dpad + 1
            s = bmp[:, start + dpad:start + dpad + SLAB_ROWS]
            row_ok = (kl >= lo) & (kl < hi) & (kl + start >= 0) & (kl + start < n_drow)
            slabs.append(jnp.where(row_ok[None, :, None, None], s, NEG_BIG))
    tab = jnp.stack(slabs, axis=1).reshape(N_HEADS, N_SLAB_TYPES, 2, SLAB_ROWS, GRID_W, GRID_W)
    tab = jnp.transpose(tab, (0, 1, 2, 4, 3, 5))
    return tab.reshape(N_HEADS, N_SLAB_TYPES, 2 * GRID_W, SLAB_ROWS * GRID_W)


def _attn_kernel(n_rows, q_ref, k_ref, v_ref, kc_ref, vc_ref, sb_ref, o_ref):
    rb = pl.program_id(2)
    base = jnp.clip(QROWS * rb - WIN_ROWS // 2, 0, n_rows - KROWS)
    tq = q_ref.shape[1]
    pr = 2 * GRID_W
    slab = SLAB_ROWS * GRID_W
    lane = lax.broadcasted_iota(i32, (tq, LANES), 1)
    q = q_ref[0]
    zero = jnp.zeros_like(q)
    q2 = jnp.concatenate([jnp.where(lane < HEAD_DIM, q, zero), jnp.where(lane >= HEAD_DIM, q, zero)], axis=0)
    sc_all = _dot_nt(q2, kc_ref[0])
    pws, pcs, dens = [], [], []
    for ip in range(QROWS // 2):
        r0 = QROWS * rb + 2 * ip
        rs0 = jnp.clip(r0 - WIN_ROWS // 2, 0, n_rows - WIN_ROWS)
        s = jnp.minimum(rs0 - base, KROWS - SLAB_ROWS)
        typ = jnp.where(r0 < 2, 1, jnp.where(r0 < 4, 2, jnp.where(r0 == n_rows - 4, 3, jnp.where(r0 == n_rows - 2, 4, 0))))
        off = pl.multiple_of(s * GRID_W, GRID_W)
        kw = k_ref[pl.ds(off, slab), :]
        ra, rbb = slice(ip * pr, (ip + 1) * pr), slice(tq + ip * pr, tq + (ip + 1) * pr)
        qq = jnp.concatenate([q2[ra], q2[rbb]], axis=0)
        sw = _dot_nt(qq, kw) + jnp.concatenate([sb_ref[0, typ], sb_ref[1, typ]], axis=0)
        sc = jnp.concatenate([sc_all[ra], sc_all[rbb]], axis=0)
        m = jnp.maximum(jnp.max(sw, axis=-1, keepdims=True), jnp.max(sc, axis=-1, keepdims=True))
        pw = jnp.exp(sw - m)
        pc = jnp.exp(sc - m)
        dens.append(jnp.sum(pw, axis=-1, keepdims=True) + jnp.sum(pc, axis=-1, keepdims=True))
        pws.append((pw.astype(bf16), off))
        pcs.append(pc.astype(bf16))
    oc = _dot(jnp.concatenate(pcs, axis=0), vc_ref[0])
    lane_p = lax.broadcasted_iota(i32, (pr, LANES), 1)
    for ip in range(QROWS // 2):
        pw, off = pws[ip]
        o = (_dot(pw, v_ref[pl.ds(off, slab), :]) + oc[ip * 2 * pr:(ip + 1) * 2 * pr]) / dens[ip]
        o_ref[0, ip * pr:(ip + 1) * pr, :] = jnp.where(lane_p < HEAD_DIM, o[:pr], o[pr:]).astype(bf16)


def _attention(q3, k3, v3, kc3, vc3, sb, batch, seq):
    n_rows = seq // GRID_W
    assert n_rows % QROWS == 0 and n_rows >= KROWS
    nrb = n_rows // QROWS
    tq = QROWS * GRID_W
    tk = KROWS * GRID_W

    n_tok = batch * seq

    def kv_map(b, hp, rb):
        base = jnp.clip(QROWS * rb - WIN_ROWS // 2, 0, n_rows - KROWS)
        return ((hp * (n_tok // GRID_W) + b * n_rows + base) * GRID_W, 0)

    kv_spec = lambda: pl.BlockSpec((pl.Element(tk), pl.Element(LANES)), kv_map)
    return pl.pallas_call(
        functools.partial(_attn_kernel, n_rows),
        grid=(batch, N_PAIRS, nrb),
        in_specs=[
            pl.BlockSpec((1, tq, LANES), lambda b, hp, rb: (hp, b * nrb + rb, 0)),
            kv_spec(), kv_spec(),
            pl.BlockSpec((1, CTX_LEN, LANES), lambda b, hp, rb: (hp, b, 0)),
            pl.BlockSpec((1, CTX_LEN, LANES), lambda b, hp, rb: (hp, b, 0)),
            pl.BlockSpec((2, N_SLAB_TYPES, 2 * GRID_W, SLAB_ROWS * GRID_W), lambda b, hp, rb: (hp, 0, 0, 0)),
        ],
        out_specs=pl.BlockSpec((1, tq, LANES), lambda b, hp, rb: (hp, b * nrb + rb, 0)),
        out_shape=jax.ShapeDtypeStruct(q3.shape, bf16),
        compiler_params=_cparams(3),
        name="attn",
    )(q3, k3.reshape(N_PAIRS * n_tok, LANES), v3.reshape(N_PAIRS * n_tok, LANES), kc3, vc3, sb)


A_CHUNK = 8


def _filter_kernel(seq, w1t_ref, w1c_ref, w1s_ref, b1_ref, fr1_ref, w2t_ref, b2_ref, fr2_ref, w3t_ref, dl_ref,
                   fq_ref, o_ref):
    st = pl.program_id(0)
    npos = A_CHUNK * LANES
    n = st * npos + lax.broadcasted_iota(i32, (1, npos), 1)
    idx = jnp.where(n < seq, n, 2 * seq - n).astype(f32)
    t = idx / (seq - 1)
    ang = (2.0 * math.pi / seq) * idx
    fa = fq_ref[...] * ang
    z1 = w1t_ref[...] * t + _dot(w1c_ref[...], jnp.cos(fa), HI) + _dot(w1s_ref[...], -jnp.sin(fa), HI) + b1_ref[...]
    h1 = jnp.sin(fr1_ref[...] * z1)
    h2 = jnp.sin(fr2_ref[...] * (_dot(w2t_ref[...], h1, HI) + b2_ref[...]))
    filt = _dot(w3t_ref[0], h2, HI) * jnp.exp(-t * dl_ref[...])
    filt = jnp.where(n == seq, 0.0, filt)
    for al in range(A_CHUNK):
        o_ref[:, al, :] = filt[:, al * LANES:(al + 1) * LANES]


def _hyena_filter(seq, w1, b1, fr1, w2, b2, fr2, w3):
    na = 2 * seq // LANES
    steps = na // A_CHUNK
    fwd_steps = steps // 2
    col = lambda v: v.reshape(-1, 1).astype(f32)
    deltas = jnp.abs(jnp.linspace(math.log(DECAY_TARGET) / SLOW_DECAY_PCT, math.log(DECAY_TARGET) / FAST_DECAY_PCT,
                                  D_B, dtype=f32))
    freqs = jnp.linspace(1e-4, N_BANDS - 1, N_BANDS, dtype=f32)
    w3t = w3.astype(f32).T.reshape(2, D_B, FILT_HID)
    w1 = w1.astype(f32)
    args = (w1[0:1].T, w1[1:1 + N_BANDS].T, w1[1 + N_BANDS:].T, col(b1), col(fr1), w2.astype(f32).T, col(b2), col(fr2),
            w3t, col(deltas), col(freqs))
    full = lambda a: pl.BlockSpec(a.shape, lambda s: (0,) * a.ndim)
    in_specs = [full(a) for a in args]
    in_specs[8] = pl.BlockSpec((1, D_B, FILT_HID), lambda s: (s // fwd_steps, 0, 0))
    return pl.pallas_call(
        functools.partial(_filter_kernel, seq),
        grid=(steps,),
        in_specs=in_specs,
        out_specs=pl.BlockSpec((D_B, A_CHUNK, LANES), lambda s: (0, s, 0)),
        out_shape=jax.ShapeDtypeStruct((D_B, na, LANES), f32),
        compiler_params=_cparams(1),
        name="filt",
    )(*args)


def _dft_consts(seq):
    n = 2 * seq
    n1 = n // LANES
    a_used = seq // LANES
    k1 = np.arange(n1)[:, None]
    a = np.arange(n1)[None, :]
    ang1 = -2.0 * np.pi * ((k1 * a) % n1) / n1
    f1 = np.concatenate([np.cos(ang1), np.sin(ang1)], axis=0)
    b = np.arange(LANES)[None, :]
    angt = -2.0 * np.pi * ((k1 * b) % n) / n
    tw_r, tw_i = np.cos(angt), np.sin(angt)
    bb = np.arange(LANES)[:, None]
    k2 = np.arange(LANES)[None, :]
    angg = -2.0 * np.pi * ((bb * k2) % LANES) / LANES
    gr, gi = np.cos(angg), np.sin(angg)
    g_fwd = np.block([[gr, gi], [-gi, gr]])
    g_inv = np.block([[gr, -gi], [gi, gr]])
    f1_inv = np.concatenate([np.cos(ang1).T[:a_used], np.sin(ang1).T[:a_used]], axis=1)
    c = lambda x: jnp.asarray(x, f32)
    return dict(n1=n1, a_used=a_used, f1_full=c(f1), f1=c(f1[:, :a_used]), tw_r=c(tw_r), tw_i=c(tw_i), g_fwd=c(g_fwd),
                g_inv=c(g_inv), f1_inv=c(f1_inv))


def _fft_fwd(x, f1, tw_r, tw_i, g_fwd, precision, cast):
    db = x.shape[0]
    n1 = tw_r.shape[0]
    f1b = jnp.broadcast_to(cast(f1)[None], (db,) + f1.shape)
    y = _bdot(f1b, cast(x), precision)
    yr, yi = y[:, :n1], y[:, n1:]
    ytr = yr * tw_r - yi * tw_i
    yti = yr * tw_i + yi * tw_r
    lhs = jnp.concatenate([ytr, yti], axis=-1).reshape(db * n1, 2 * LANES)
    return _dot(cast(lhs), cast(g_fwd), precision)


def _fftk_kernel(x_ref, f1_ref, twr_ref, twi_ref, g_ref, o_ref):
    db, n1 = x_ref.shape[0], twr_ref.shape[0]
    z = _fft_fwd(x_ref[...], f1_ref[...], twr_ref[...], twi_ref[...], g_ref[...], HI, lambda v: v)
    o_ref[...] = z.reshape(db, n1, 2 * LANES)


def _filter_spectrum(kf, dc):
    d, n1, _ = kf.shape
    db = 16
    full = lambda a: pl.BlockSpec(a.shape, lambda i: (0,) * a.ndim)
    consts = (dc["f1_full"], dc["tw_r"], dc["tw_i"], dc["g_fwd"])
    return pl.pallas_call(
        _fftk_kernel,
        grid=(d // db,),
        in_specs=[pl.BlockSpec((db, n1, LANES), lambda i: (i, 0, 0))] + [full(a) for a in consts],
        out_specs=pl.BlockSpec((db, n1, 2 * LANES), lambda i: (i, 0, 0)),
        out_shape=jax.ShapeDtypeStruct((d, n1, 2 * LANES), f32),
        compiler_params=_cparams(1),
        name="fftk",
    )(kf, *consts)


def _hpre_kernel(n_chunks, u_ref, up_ref, un_ref, cw_ref, cb_ref, zt_ref, x0_ref):
    ac = pl.program_id(1)
    tm = u_ref.shape[0]
    has_prev = (ac > 0).astype(f32)
    has_next = (ac < n_chunks - 1).astype(f32)
    row = lax.broadcasted_iota(i32, (tm, LANES), 0)
    halo = up_ref.shape[0]

    def conv(c0):
        sl = slice(c0, c0 + LANES)
        u = u_ref[:, sl].astype(f32)
        prev = up_ref[halo - 1:halo, sl].astype(f32) * has_prev
        nxt = un_ref[0:1, sl].astype(f32) * has_next
        um = jnp.where(row == 0, prev, pltpu.roll(u, 1, 0))
        upl = jnp.where(row == tm - 1, nxt, pltpu.roll(u, tm - 1, 0))
        return um * cw_ref[0:1, sl] + u * cw_ref[1:2, sl] + upl * cw_ref[2:3, sl] + cb_ref[:, sl]

    for dt in range(D_B // LANES):
        v = conv(dt * LANES)
        x1 = conv(D_B + dt * LANES)
        x0_ref[:, dt * LANES:(dt + 1) * LANES] = conv(2 * D_B + dt * LANES).astype(bf16)
        z = v * x1
        for al in range(A_CHUNK):
            zt_ref[0, dt * LANES:(dt + 1) * LANES, al, :] = z[al * LANES:(al + 1) * LANES, :].T


def _hyena_pre(u, conv_w, conv_b, batch, seq):
    n = u.shape[0]
    tm = A_CHUNK * LANES
    n_chunks = seq // tm
    halo = 16
    hb = tm // halo
    cw = jnp.concatenate([conv_w.astype(f32), jnp.zeros((8 - conv_w.shape[0], conv_w.shape[1]), f32)], axis=0)
    return pl.pallas_call(
        functools.partial(_hpre_kernel, n_chunks),
        grid=(batch, n_chunks),
        in_specs=[
            pl.BlockSpec((tm, 3 * D_B), lambda b, c: (b * n_chunks + c, 0)),
            pl.BlockSpec((halo, 3 * D_B), lambda b, c: (jnp.maximum((b * n_chunks + c) * hb - 1, 0), 0)),
            pl.BlockSpec((halo, 3 * D_B), lambda b, c: (jnp.minimum((b * n_chunks + c + 1) * hb, n // halo - 1), 0)),
            pl.BlockSpec((8, 3 * D_B), lambda b, c: (0, 0)),
            pl.BlockSpec((1, 3 * D_B), lambda b, c: (0, 0)),
        ],
        out_specs=[
            pl.BlockSpec((1, D_B, A_CHUNK, LANES), lambda b, c: (b, 0, c, 0)),
            pl.BlockSpec((tm, D_B), lambda b, c: (b * n_chunks + c, 0)),
        ],
        out_shape=[
            jax.ShapeDtypeStruct((batch, D_B, seq // LANES, LANES), f32),
            jax.ShapeDtypeStruct((n, D_B), bf16),
        ],
        compiler_params=_cparams(2),
        name="hpre",
    )(u, u, u, cw, conv_b.reshape(1, -1).astype(f32))


def _fftconv_kernel(z_ref, kh_ref, skip_ref, f1_ref, twr_ref, twi_ref, gf_ref, gi_ref, f1i_ref, y_ref):
    db, a_used = z_ref.shape[1], z_ref.shape[2]
    n1 = twr_ref.shape[0]
    tw_r, tw_i = twr_ref[...], twi_ref[...]
    cast = lambda v: v.astype(bf16)
    z = z_ref[0]
    spec = _fft_fwd(z, f1_ref[...], tw_r, tw_i, gf_ref[...], None, cast)
    kh = kh_ref[...].reshape(db * n1, 2 * LANES)
    sr, si = spec[:, :LANES], spec[:, LANES:]
    kr, ki = kh[:, :LANES], kh[:, LANES:]
    prod = jnp.concatenate([sr * kr - si * ki, sr * ki + si * kr], axis=-1)
    q = _dot(cast(prod), cast(gi_ref[...])).reshape(db, n1, 2 * LANES)
    qr, qi = q[:, :, :LANES], q[:, :, LANES:]
    rhs = jnp.concatenate([qr * tw_r + qi * tw_i, qi * tw_r - qr * tw_i], axis=1)
    f1ib = jnp.broadcast_to(cast(f1i_ref[...])[None], (db, a_used, 2 * n1))
    y = _bdot(f1ib, cast(rhs)) * (1.0 / (n1 * LANES))
    y_ref[0] = y + z * skip_ref[...]


def _fftconv(zt, khat, skip, dc):
    batch, d, a_used, _ = zt.shape
    n1 = dc["n1"]
    db = 16
    consts = (dc["f1"], dc["tw_r"], dc["tw_i"], dc["g_fwd"], dc["g_inv"], dc["f1_inv"])
    full = lambda a: pl.BlockSpec(a.shape, lambda i, b: (0,) * a.ndim)
    return pl.pallas_call(
        _fftconv_kernel,
        grid=(d // db, batch),
        in_specs=[
            pl.BlockSpec((1, db, a_used, LANES), lambda i, b: (b, i, 0, 0)),
            pl.BlockSpec((db, n1, 2 * LANES), lambda i, b: (i, 0, 0)),
            pl.BlockSpec((db, 1, 1), lambda i, b: (i, 0, 0)),
        ] + [full(a) for a in consts],
        out_specs=pl.BlockSpec((1, db, a_used, LANES), lambda i, b: (b, i, 0, 0)),
        out_shape=jax.ShapeDtypeStruct(zt.shape, f32),
        compiler_params=_cparams(2),
        name="fftconv",
    )(zt, khat, skip.reshape(d, 1, 1).astype(f32), *consts)


def _hpost_kernel(yt_ref, x0_ref, ob_ref):
    for dt in range(D_B // LANES):
        for al in range(A_CHUNK):
            rows = slice(al * LANES, (al + 1) * LANES)
            cols = slice(dt * LANES, (dt + 1) * LANES)
            y = yt_ref[0, cols, al, :].T
            ob_ref[rows, cols] = (x0_ref[rows, cols].astype(f32) * y).astype(bf16)


def _hyena_post(yt, x0, batch, seq):
    tm = A_CHUNK * LANES
    n_chunks = seq // tm
    return pl.pallas_call(
        _hpost_kernel,
        grid=(batch, n_chunks),
        in_specs=[
            pl.BlockSpec((1, D_B, A_CHUNK, LANES), lambda b, c: (b, 0, c, 0)),
            pl.BlockSpec((tm, D_B), lambda b, c: (b * n_chunks + c, 0)),
        ],
        out_specs=pl.BlockSpec((tm, D_B), lambda b, c: (b * n_chunks + c, 0)),
        out_shape=jax.ShapeDtypeStruct(x0.shape, bf16),
        compiler_params=_cparams(2),
        name="hpost",
    )(yt, x0)


IDX_ROWS = 8


def _merge_kernel(oa_ref, ob_ref, ga_ref, gb_ref, x_ref, g1_ref, wba_ref, wbb_ref, wout_ref, n2g_ref, sh2_ref, sc2_ref,
                  wrt_ref, br_ref, x1_ref, h2_ref, idx_ref, wt_ref):
    oa = jnp.concatenate([oa_ref[hp] for hp in range(N_PAIRS)], axis=1)
    m = ga_ref[...].astype(f32) * _dot(oa, wba_ref[...]) + gb_ref[...].astype(f32) * _dot(ob_ref[...], wbb_ref[...])
    x1 = x_ref[...] + g1_ref[0] * _dot(m.astype(bf16), wout_ref[...])
    x1_ref[...] = x1
    ms = jnp.mean(x1 * x1, axis=-1, keepdims=True)
    h2 = (x1 * lax.rsqrt(ms + RMS_EPS) * n2g_ref[...]) * (1.0 + sc2_ref[0]) + sh2_ref[0]
    h2_ref[...] = h2.astype(bf16)
    logits = _dot_nt(wrt_ref[...], h2, HI) + br_ref[...]
    eio = lax.broadcasted_iota(i32, logits.shape, 0)
    vals = logits
    idxs, tops = [], []
    for _ in range(TOP_K):
        mx = jnp.max(vals, axis=0, keepdims=True)
        ix = jnp.min(jnp.where(vals == mx, eio, N_EXPERTS), axis=0, keepdims=True)
        idxs.append(ix)
        tops.append(mx)
        vals = jnp.where(eio == ix, -jnp.inf, vals)
    ex = [jnp.exp(v - tops[0]) for v in tops]
    den = ex[0] + ex[1] + ex[2] + ex[3]
    idx_ref[...] = jnp.concatenate(idxs + [jnp.full_like(idxs[0], -1)] * (IDX_ROWS - TOP_K), axis=0)
    wt_ref[...] = jnp.concatenate([e / den for e in ex] + [jnp.zeros_like(den)] * (IDX_ROWS - TOP_K), axis=0)


def _merge(oa3, ob, ga, gb, x2, g1, wba, wbb, wout, n2g, sh2, sc2, wrt, br, tm, tiles_per_batch):
    n, d = x2.shape
    tok = lambda: pl.BlockSpec((tm, d), lambda i: (i, 0))
    mod = lambda: pl.BlockSpec((1, 1, d), lambda i: (i // tiles_per_batch, 0, 0))
    wsp = lambda: pl.BlockSpec((d, d), lambda i: (0, 0))
    return pl.pallas_call(
        _merge_kernel,
        grid=(n // tm,),
        in_specs=[
            pl.BlockSpec((N_PAIRS, tm, LANES), lambda i: (0, i, 0)),
            tok(), tok(), tok(), tok(), mod(), wsp(), wsp(), wsp(),
            pl.BlockSpec((1, d), lambda i: (0, 0)), mod(), mod(),
            pl.BlockSpec((N_EXPERTS, d), lambda i: (0, 0)),
            pl.BlockSpec((N_EXPERTS, 1), lambda i: (0, 0)),
        ],
        out_specs=[tok(), tok(), pl.BlockSpec((IDX_ROWS, tm), lambda i: (0, i)), pl.BlockSpec((IDX_ROWS, tm), lambda i: (0, i))],
        out_shape=[
            jax.ShapeDtypeStruct((n, d), f32),
            jax.ShapeDtypeStruct((n, d), bf16),
            jax.ShapeDtypeStruct((IDX_ROWS, n), i32),
            jax.ShapeDtypeStruct((IDX_ROWS, n), f32),
        ],
        compiler_params=_cparams(1),
        name="merge",
    )(oa3, ob, ga, gb, x2, g1, wba, wbb, wout, n2g, sh2, sc2, wrt, br)


MOE_SUB = 1024
MOE_NSUB = 2
MOE_TILE = MOE_SUB * MOE_NSUB
MOE_ROWS = 160
MOE_CAP = 12 * MOE_ROWS
MOE_MAX_CHUNKS = -(-MOE_SUB // MOE_ROWS)


def _moe_kernel(h2_ref, idx_ref, wt_ref, wgu_ref, bgu_ref, wd_ref, bd_ref, y_ref, rank_scr, msk_scr, wte_scr, pos_scr,
                o_scr, st_ref):
    e = pl.program_id(1)
    ts = MOE_SUB

    @pl.when(e == 0)
    def _():
        y_ref[...] = jnp.zeros_like(y_ref)
        o_scr[...] = jnp.zeros_like(o_scr)
        eio = lax.broadcasted_iota(i32, (N_EXPERTS, ts), 0)
        lane = lax.broadcasted_iota(i32, (N_EXPERTS, ts), 1)
        er = lax.broadcasted_iota(i32, (N_EXPERTS, N_EXPERTS), 0)
        ec = lax.broadcasted_iota(i32, (N_EXPERTS, N_EXPERTS), 1)
        lower = (er > ec).astype(f32)
        for s in range(MOE_NSUB):
            idx = idx_ref[:, s * ts:(s + 1) * ts]
            w = wt_ref[:, s * ts:(s + 1) * ts]
            hits = [idx[k:k + 1, :] == eio for k in range(TOP_K)]
            msk = jnp.zeros((N_EXPERTS, ts), f32)
            wte = jnp.zeros((N_EXPERTS, ts), f32)
            for k in range(TOP_K):
                msk = msk + hits[k].astype(f32)
                wte = wte + jnp.where(hits[k], w[k:k + 1, :], 0.0)
            csum = msk
            sh = 1
            while sh < ts:
                csum = csum + jnp.where(lane >= sh, pltpu.roll(csum, sh, 1), 0.0)
                sh *= 2
            rank = csum - msk
            cnt = csum[:, ts - 1:ts]
            nch = jnp.zeros_like(cnt)
            for m in range(MOE_MAX_CHUNKS):
                nch = nch + (cnt > float(m * MOE_ROWS)).astype(f32)
            cbase = _dot(lower, jnp.broadcast_to(nch, (N_EXPERTS, LANES)), HI)[:, :1]
            gpos = cbase * float(MOE_ROWS) + rank
            rows = [jnp.sum(jnp.where(hits[k], gpos, 0.0), axis=0, keepdims=True).astype(i32) for k in range(TOP_K)]
            pos_scr[s] = jnp.concatenate(rows + [jnp.full((IDX_ROWS - TOP_K, ts), -1, i32)], axis=0)
            rank_scr[s] = rank.astype(i32)
            msk_scr[s] = msk
            wte_scr[s] = wte
            st_ref[2 * s] = 0
            st_ref[2 * s + 1] = 0

    for s in range(MOE_NSUB):
        tok = slice(s * ts, (s + 1) * ts)

        def flush(s=s, tok=tok):
            rel = pos_scr[s] - st_ref[2 * s + 1]
            pio = lax.broadcasted_iota(i32, (MOE_CAP, ts), 0)
            g = rel[0:1, :] == pio
            for k in range(1, TOP_K):
                g = g | (rel[k:k + 1, :] == pio)
            y_ref[tok, :] += _dot_tn(g.astype(bf16), o_scr[s])
            o_scr[s] = jnp.zeros((MOE_CAP, D_MODEL), bf16)
            st_ref[2 * s + 1] = st_ref[2 * s + 1] + st_ref[2 * s]
            st_ref[2 * s] = 0

        rank = rank_scr[s, pl.ds(e, 1), :]
        mrow = msk_scr[s, pl.ds(e, 1), :]
        sel = mrow > 0.0
        wte = wte_scr[s, pl.ds(e, 1), :]
        n_chunks = (jnp.sum(mrow).astype(i32) + MOE_ROWS - 1) // MOE_ROWS

        def chunk(jc, carry, s=s, tok=tok, rank=rank, sel=sel, wte=wte, flush=flush):
            pl.when(st_ref[2 * s] + MOE_ROWS > MOE_CAP)(flush)
            fill = st_ref[2 * s]
            jio = lax.broadcasted_iota(i32, (MOE_ROWS, ts), 0)
            hit = ((rank - jc * MOE_ROWS) == jio) & sel
            xg = _dot(hit.astype(bf16), h2_ref[tok, :]).astype(bf16)
            gu = _dot(xg, wgu_ref[0]) + bgu_ref[0]
            gate = jnp.minimum(gu[:, :D_FF], SWIGLU_LIMIT)
            up = jnp.clip(gu[:, D_FF:], -SWIGLU_LIMIT, SWIGLU_LIMIT)
            act = (up + 1.0) * gate * _sigmoid(SWIGLU_ALPHA * gate)
            o = _dot(act.astype(bf16), wd_ref[0]) + bd_ref[0]
            wrow = jnp.sum(jnp.where(hit, wte, 0.0), axis=1, keepdims=True)
            o_scr[s, pl.ds(pl.multiple_of(fill, 16), MOE_ROWS), :] = (o * wrow).astype(bf16)
            st_ref[2 * s] = fill + MOE_ROWS
            return carry

        lax.fori_loop(0, n_chunks, chunk, 0)
        pl.when(e == N_EXPERTS - 1)(flush)


def _moe(h2, idx, wt, wgu, bgu, wd, bd):
    n, d = h2.shape
    t = MOE_TILE
    once = dict(pipeline_mode=pl.Buffered(1))
    return pl.pallas_call(
        _moe_kernel,
        grid=(n // t, N_EXPERTS),
        in_specs=[
            pl.BlockSpec((t, d), lambda i, e: (i, 0), **once),
            pl.BlockSpec((IDX_ROWS, t), lambda i, e: (0, i), **once),
            pl.BlockSpec((IDX_ROWS, t), lambda i, e: (0, i), **once),
            pl.BlockSpec((1, d, 2 * D_FF), lambda i, e: (e, 0, 0)),
            pl.BlockSpec((1, 1, 2 * D_FF), lambda i, e: (e, 0, 0)),
            pl.BlockSpec((1, D_FF, d), lambda i, e: (e, 0, 0)),
            pl.BlockSpec((1, 1, d), lambda i, e: (e, 0, 0)),
        ],
        out_specs=pl.BlockSpec((t, d), lambda i, e: (i, 0), **once),
        out_shape=jax.ShapeDtypeStruct((n, d), f32),
        scratch_shapes=[
            pltpu.VMEM((MOE_NSUB, N_EXPERTS, MOE_SUB), i32),
            pltpu.VMEM((MOE_NSUB, N_EXPERTS, MOE_SUB), f32),
            pltpu.VMEM((MOE_NSUB, N_EXPERTS, MOE_SUB), f32),
            pltpu.VMEM((MOE_NSUB, IDX_ROWS, MOE_SUB), i32),
            pltpu.VMEM((MOE_NSUB, MOE_CAP, d), bf16),
            pltpu.SMEM((2 * MOE_NSUB,), i32),
        ],
        compiler_params=_cparams(2),
        name="moe",
    )(h2, idx, wt, wgu, bgu, wd, bd)


def _final_kernel(x1_ref, y_ref, g2_ref, o_ref):
    o_ref[...] = x1_ref[...] + g2_ref[0] * y_ref[...]


def _final(x1, y, g2, tm, tiles_per_batch):
    n, d = x1.shape
    tok = lambda: pl.BlockSpec((tm, d), lambda i: (i, 0))
    return pl.pallas_call(
        _final_kernel,
        grid=(n // tm,),
        in_specs=[tok(), tok(), pl.BlockSpec((1, 1, d), lambda i: (i // tiles_per_batch, 0, 0))],
        out_specs=tok(),
        out_shape=jax.ShapeDtypeStruct((n, d), f32),
        compiler_params=_cparams(1),
        name="final",
    )(x1, y, g2)


def _layer(x, c, ctx, c_ctx, w_ada, b_ada, norm1_g, norm2_g, w_in, q_norm_g, k_norm_g, rpb, conv_w, conv_b, filt_w1,
           filt_b1, filt_freq1, filt_w2, filt_b2, filt_freq2, filt_w3, hyena_skip, w_branch_a, w_branch_b, w_out,
           w_router, b_router, w_gate_up, b_gate_up, w_down, b_down):
    batch, seq, d = x.shape
    n = batch * seq
    n_ctx = ctx.shape[1]
    assert d == D_MODEL and n_ctx == CTX_LEN and seq % (A_CHUNK * LANES) == 0 and n % MOE_TILE == 0

    rows = 8 * ((batch + 1 + 7) // 8)
    cv = jnp.zeros((rows, d), f32).at[:batch].set(c).at[batch].set(c_ctx)
    mod = _adaln(cv, w_ada, b_ada.reshape(1, -1))
    sh1, sc1, g1, sh2, sc2, g2 = [mod[:batch, i * d:(i + 1) * d].reshape(batch, 1, d) for i in range(N_MOD)]
    modc = jnp.broadcast_to(mod[batch:batch + 1], (batch, N_MOD * d))
    sh1c, sc1c = modc[:, :d].reshape(batch, 1, d), modc[:, d:2 * d].reshape(batch, 1, d)

    w_in_bf = w_in.astype(bf16)
    ones = jnp.asarray(np.kron(np.eye(2), np.ones((HEAD_DIM, HEAD_DIM))), bf16)
    qg = (jnp.tile(q_norm_g.astype(f32), N_HEADS) * (HEAD_DIM ** -0.5)).reshape(1, d)
    kg = jnp.tile(k_norm_g.astype(f32), N_HEADS).reshape(1, d)
    n1g = norm1_g.reshape(1, d)

    kc3, vc3 = _inproj(ctx.reshape(batch * n_ctx, d), n1g, sh1c, sc1c, w_in_bf, qg, kg, ones,
                       (("k", 0), ("v", 1)), 1, n_ctx, 1)
    tm = 512
    x2 = x.reshape(n, d)
    kinds = (("q", 0), ("k", 1), ("v", 2), ("u", 3), ("u", 3), ("u", 3), ("g", 4), ("g", 5))
    tm_in = 1024
    q3, k3, v3, u, ga, gb = _inproj(x2, n1g, sh1, sc1, w_in_bf, qg, kg, ones, kinds, 0, tm_in, seq // tm_in)

    oa3 = _attention(q3, k3, v3, kc3, vc3, _slab_bias_tables(rpb), batch, seq)

    dc = _dft_consts(seq)
    kf = _hyena_filter(seq, filt_w1, filt_b1, filt_freq1, filt_w2, filt_b2, filt_freq2, filt_w3)
    khat = _filter_spectrum(kf, dc)
    zt, x0 = _hyena_pre(u, conv_w, conv_b, batch, seq)
    yt = _fftconv(zt, khat, hyena_skip, dc)
    ob = _hyena_post(yt, x0, batch, seq)

    x1, h2, idx, wt = _merge(oa3, ob, ga, gb, x2, g1, w_branch_a.astype(bf16), w_branch_b.astype(bf16),
                             w_out.astype(bf16), norm2_g.reshape(1, d), sh2, sc2, w_router.astype(f32).T,
                             b_router.reshape(-1, 1).astype(f32), tm, seq // tm)
    y = _moe(h2, idx, wt, w_gate_up.astype(bf16), b_gate_up.reshape(N_EXPERTS, 1, -1).astype(f32),
             w_down.astype(bf16), b_down.reshape(N_EXPERTS, 1, -1).astype(f32))
    out = _final(x1, y, g2, tm, seq // tm)
    return out.reshape(batch, seq, d)


def kernel(x, c, ctx, c_ctx, w_ada, b_ada, norm1_g, norm2_g, w_in, q_norm_g, k_norm_g, rpb, conv_w, conv_b, filt_w1, filt_b1, filt_freq1, filt_w2, filt_b2, filt_freq2, filt_w3, hyena_skip, w_branch_a, w_branch_b, w_out, w_router, b_router, w_gate_up, b_gate_up, w_down, b_down):
    assert w_ada.shape[0] == 1, "single-layer stack"
    return _layer(x, c, ctx, c_ctx, w_ada[0], b_ada[0], norm1_g[0], norm2_g[0], w_in[0], q_norm_g[0], k_norm_g[0],
                  rpb[0], conv_w[0], conv_b[0], filt_w1[0], filt_b1[0], filt_freq1[0], filt_w2[0], filt_b2[0],
                  filt_freq2[0], filt_w3[0], hyena_skip[0], w_branch_a[0], w_branch_b[0], w_out[0], w_router[0],
                  b_router[0], w_gate_up[0], b_gate_up[0], w_down[0], b_down[0])
```

```python
import functools
import math

import numpy as np
import jax
import jax.numpy as jnp
from jax import lax
from jax.experimental import pallas as pl
from jax.experimental.pallas import tpu as pltpu

f32, bf16, i32 = jnp.float32, jnp.bfloat16, jnp.int32
HI = lax.Precision.HIGHEST

LANES = 128
D_MODEL = 1024
N_HEADS = 16
HEAD_DIM = 64
N_PAIRS = N_HEADS // 2
GRID_W = 64
WIN_ROWS = 8
WIN_COLS = 16
CTX_LEN = 256
N_MOD = 6
RMS_EPS = 1e-6
D_B = 1024
EMB_DIM = 33
FILT_HID = 64
N_BANDS = (EMB_DIM - 1) // 2
DECAY_TARGET = 1e-2
FAST_DECAY_PCT = 0.3
SLOW_DECAY_PCT = 1.5
N_EXPERTS = 32
TOP_K = 4
D_FF = 1024
SWIGLU_LIMIT = 7.0
SWIGLU_ALPHA = 1.702
NEG_BIG = -1e30

QROWS = 8
KROWS = 16
SLAB_ROWS = 10
N_SLAB_TYPES = 5
VMEM_LIMIT = 56 * 1024 * 1024


def _cparams(ndim, vmem=VMEM_LIMIT):
    return pltpu.CompilerParams(dimension_semantics=("arbitrary",) * ndim, vmem_limit_bytes=vmem)


def _sigmoid(x):
    return 1.0 / (1.0 + jnp.exp(-x))


def _dot(a, b, precision=None):
    return jnp.dot(a, b, precision=precision, preferred_element_type=f32)


def _dot_nt(a, b, precision=None):
    return lax.dot_general(a, b, (((1,), (1,)), ((), ())), precision=precision, preferred_element_type=f32)


def _dot_tn(a, b):
    return lax.dot_general(a, b, (((0,), (0,)), ((), ())), preferred_element_type=f32)


def _bdot(a, b, precision=None):
    return lax.dot_general(a, b, (((2,), (1,)), ((0,), (0,))), precision=precision, preferred_element_type=f32)


def _adaln_kernel(cv_ref, w_ref, b_ref, o_ref):
    cv = cv_ref[...]
    o_ref[...] = _dot(cv * _sigmoid(cv), w_ref[...], HI) + b_ref[...]


def _adaln(cv, w, b):
    rows, d = cv.shape
    n = w.shape[1]
    tn = 1024
    return pl.pallas_call(
        _adaln_kernel,
        grid=(n // tn,),
        in_specs=[
            pl.BlockSpec((rows, d), lambda j: (0, 0)),
            pl.BlockSpec((d, tn), lambda j: (0, j)),
            pl.BlockSpec((1, tn), lambda j: (0, j)),
        ],
        out_specs=pl.BlockSpec((rows, tn), lambda j: (0, j)),
        out_shape=jax.ShapeDtypeStruct((rows, n), f32),
        compiler_params=_cparams(1),
        name="adaln",
    )(cv, w, b)


def _head_norm(acc, gain, ones):
    s2 = acc * acc
    hi = s2.astype(bf16)
    lo = (s2 - hi.astype(f32)).astype(bf16)
    ssum = _dot(hi, ones) + _dot(lo, ones)
    return acc * lax.rsqrt(ssum * (1.0 / HEAD_DIM) + RMS_EPS) * gain


def _inproj_kernel(n_norm, n_hm, n_u, n_g, x_ref, g_ref, sh_ref, sc_ref, w_ref, ones_ref, *rest):
    gains, rest = rest[:n_norm], list(rest[n_norm:])
    hm_ref = rest.pop(0)
    u_ref = rest.pop(0) if n_u else None
    gate_ref = rest.pop(0) if n_g else None
    h_scr = rest.pop(0)
    j = pl.program_id(1)

    @pl.when(j == 0)
    def _():
        xf = x_ref[...]
        ms = jnp.mean(xf * xf, axis=-1, keepdims=True)
        y = xf * lax.rsqrt(ms + RMS_EPS) * g_ref[...]
        h_scr[...] = (y * (1.0 + sc_ref[0]) + sh_ref[0]).astype(bf16)

    project = lambda: _dot(h_scr[...], w_ref[...])

    @pl.when(j < n_norm)
    def _():
        acc = project()
        gain = gains[n_norm - 1][...]
        for t in range(n_norm - 2, -1, -1):
            gain = jnp.where(j == t, gains[t][...], gain)
        for hp in range(N_PAIRS):
            sl = slice(hp * LANES, (hp + 1) * LANES)
            hm_ref[0, hp] = _head_norm(acc[:, sl], gain[:, sl], ones_ref[...]).astype(bf16)

    if n_hm > n_norm:

        @pl.when((j >= n_norm) & (j < n_hm))
        def _():
            acc = project()
            for hp in range(N_PAIRS):
                hm_ref[0, hp] = acc[:, hp * LANES:(hp + 1) * LANES].astype(bf16)

    if n_u:

        @pl.when((j >= n_hm) & (j < n_hm + n_u))
        def _():
            u_ref[...] = project().astype(bf16)

    if n_g:

        @pl.when(j >= n_hm + n_u)
        def _():
            gate_ref[...] = _sigmoid(project()).astype(bf16)


def _inproj(x2, gain, shift, scale, w_bf, norm_gains, ones, n_hm, n_u, n_g, col0, tm, tiles_per_batch):
    n, d = x2.shape
    n_norm = len(norm_gains)
    out_shapes = [jax.ShapeDtypeStruct((n_hm, N_PAIRS, n, LANES), bf16)]
    out_specs = [pl.BlockSpec((1, N_PAIRS, tm, LANES), lambda i, j: (jnp.clip(j, 0, n_hm - 1), 0, i, 0))]
    for j0, ncol in ((n_hm, n_u), (n_hm + n_u, n_g)):
        if ncol:
            out_shapes.append(jax.ShapeDtypeStruct((n, d * ncol), bf16))
            out_specs.append(pl.BlockSpec((tm, d), lambda i, j, j0=j0, ncol=ncol: (i, jnp.clip(j - j0, 0, ncol - 1))))
    vec = lambda: pl.BlockSpec((1, d), lambda i, j: (0, 0))
    mod = lambda: pl.BlockSpec((1, 1, d), lambda i, j: (i // tiles_per_batch, 0, 0))
    return pl.pallas_call(
        functools.partial(_inproj_kernel, n_norm, n_hm, n_u, n_g),
        grid=(n // tm, n_hm + n_u + n_g),
        in_specs=[
            pl.BlockSpec((tm, d), lambda i, j: (i, 0)),
            vec(), mod(), mod(),
            pl.BlockSpec((d, d), lambda i, j: (0, j + col0)),
            pl.BlockSpec((LANES, LANES), lambda i, j: (0, 0)),
        ] + [vec() for _ in norm_gains],
        out_specs=out_specs,
        out_shape=out_shapes,
        scratch_shapes=[pltpu.VMEM((tm, d), bf16)],
        compiler_params=_cparams(2),
        name="inproj",
    )(x2, gain, shift, scale, w_bf, ones, *norm_gains)


def _slab_bias_tables(rpb):
    n_drow = 2 * WIN_ROWS - 1
    c = np.arange(GRID_W)
    cs = np.clip(c - WIN_COLS // 2, 0, GRID_W - WIN_COLS)
    j = np.arange(GRID_W)
    col_ok = (j[None, :] >= cs[:, None]) & (j[None, :] < cs[:, None] + WIN_COLS)
    pad = GRID_W - WIN_COLS
    rp = jnp.pad(rpb.astype(f32), ((0, 0), (0, 0), (pad, pad)))
    bm = jnp.stack([rp[:, :, pad + WIN_COLS - 1 - ci:pad + WIN_COLS - 1 - ci + GRID_W] for ci in range(GRID_W)], axis=2)
    bm = jnp.where(col_ok[None, None], bm, NEG_BIG)
    spec = {0: [(3, 0, 8), (2, 1, 9)], 1: [(7, 0, 8), (6, 0, 8)], 2: [(5, 0, 8), (4, 0, 8)],
            3: [(1, 2, 10), (0, 2, 10)], 4: [(-1, 2, 10), (-2, 2, 10)]}
    dpad = 2
    bmp = jnp.pad(bm, ((0, 0), (dpad, dpad + 1), (0, 0), (0, 0)), constant_values=NEG_BIG)
    kl = np.arange(SLAB_ROWS)
    slabs = []
    for t in range(N_SLAB_TYPES):
        for start, lo, hi in spec[t]:
            assert start + dpad >= 0 and start + dpad + SLAB_ROWS <= n_drow + 2 * dpad + 1
            s = bmp[:, start + dpad:start + dpad + SLAB_ROWS]
            row_ok = (kl >= lo) & (kl < hi) & (kl + start >= 0) & (kl + start < n_drow)
            slabs.append(jnp.where(row_ok[None, :, None, None], s, NEG_BIG))
    tab = jnp.stack(slabs, axis=1).reshape(N_HEADS, N_SLAB_TYPES, 2, SLAB_ROWS, GRID_W, GRID_W)
    tab = jnp.transpose(tab, (0, 1, 2, 4, 3, 5))
    return tab.reshape(N_HEADS, N_SLAB_TYPES, 2 * GRID_W, SLAB_ROWS * GRID_W)


ATTN_PAIRS = 2


def _attn_kernel(n_rows, q_ref, *refs):
    k_refs, v_refs = refs[:ATTN_PAIRS], refs[ATTN_PAIRS:2 * ATTN_PAIRS]
    kc_ref, vc_ref, sb_ref, o_ref = refs[2 * ATTN_PAIRS:]
    for g in range(ATTN_PAIRS):
        _attn_head_pair(n_rows, g, q_ref, k_refs[g], v_refs[g], kc_ref, vc_ref, sb_ref, o_ref)


def _attn_head_pair(n_rows, g, q_ref, k_ref, v_ref, kc_ref, vc_ref, sb_ref, o_ref):
    rb = pl.program_id(2)
    base = jnp.clip(QROWS * rb - WIN_ROWS // 2, 0, n_rows - KROWS)
    tq = q_ref.shape[2]
    pr = 2 * GRID_W
    slab = SLAB_ROWS * GRID_W
    lane = lax.broadcasted_iota(i32, (tq, LANES), 1)
    q = q_ref[0, g]
    zero = jnp.zeros_like(q)
    q2 = jnp.concatenate([jnp.where(lane < HEAD_DIM, q, zero), jnp.where(lane >= HEAD_DIM, q, zero)], axis=0)
    sc_all = _dot_nt(q2, kc_ref[0, g])
    pws, pcs, dens = [], [], []
    for ip in range(QROWS // 2):
        r0 = QROWS * rb + 2 * ip
        rs0 = jnp.clip(r0 - WIN_ROWS // 2, 0, n_rows - WIN_ROWS)
        s = jnp.minimum(rs0 - base, KROWS - SLAB_ROWS)
        typ = jnp.where(r0 < 2, 1, jnp.where(r0 < 4, 2, jnp.where(r0 == n_rows - 4, 3, jnp.where(r0 == n_rows - 2, 4, 0))))
        off = pl.multiple_of(s * GRID_W, GRID_W)
        kw = k_ref[pl.ds(off, slab), :]
        ra, rbb = slice(ip * pr, (ip + 1) * pr), slice(tq + ip * pr, tq + (ip + 1) * pr)
        qq = jnp.concatenate([q2[ra], q2[rbb]], axis=0)
        sw = _dot_nt(qq, kw) + jnp.concatenate([sb_ref[2 * g, typ], sb_ref[2 * g + 1, typ]], axis=0)
        sc = jnp.concatenate([sc_all[ra], sc_all[rbb]], axis=0)
        m = jnp.maximum(jnp.max(sw, axis=-1, keepdims=True), jnp.max(sc, axis=-1, keepdims=True))
        pw = jnp.exp(sw - m)
        pc = jnp.exp(sc - m)
        dens.append(jnp.sum(pw, axis=-1, keepdims=True) + jnp.sum(pc, axis=-1, keepdims=True))
        pws.append((pw.astype(bf16), off))
        pcs.append(pc.astype(bf16))
    oc = _dot(jnp.concatenate(pcs, axis=0), vc_ref[0, g])
    lane_p = lax.broadcasted_iota(i32, (pr, LANES), 1)
    for ip in range(QROWS // 2):
        pw, off = pws[ip]
        o = (_dot(pw, v_ref[pl.ds(off, slab), :]) + oc[ip * 2 * pr:(ip + 1) * 2 * pr]) / dens[ip]
        o_ref[g, ip * pr:(ip + 1) * pr, :] = jnp.where(lane_p < HEAD_DIM, o[:pr], o[pr:]).astype(bf16)


def _attention(qkv, ckv, sb, batch, seq):
    n_rows = seq // GRID_W
    assert n_rows % QROWS == 0 and n_rows >= KROWS
    nrb = n_rows // QROWS
    tq = QROWS * GRID_W
    tk = KROWS * GRID_W
    n_tok = batch * seq

    def kv_map(which, g):
        def index(b, hg, rb):
            base = jnp.clip(QROWS * rb - WIN_ROWS // 2, 0, n_rows - KROWS)
            hp = hg * ATTN_PAIRS + g
            return (((which * N_PAIRS + hp) * (n_tok // GRID_W) + b * n_rows + base) * GRID_W, 0)
        return index

    kv_specs = lambda which: [pl.BlockSpec((pl.Element(tk), pl.Element(LANES)), kv_map(which, g))
                              for g in range(ATTN_PAIRS)]
    ctx_spec = lambda which: pl.BlockSpec((1, ATTN_PAIRS, CTX_LEN, LANES), lambda b, hg, rb: (which, hg, b, 0))
    flat = qkv.reshape(3 * N_PAIRS * n_tok, LANES)
    return pl.pallas_call(
        functools.partial(_attn_kernel, n_rows),
        grid=(batch, N_PAIRS // ATTN_PAIRS, nrb),
        in_specs=[pl.BlockSpec((1, ATTN_PAIRS, tq, LANES), lambda b, hg, rb: (0, hg, b * nrb + rb, 0))]
        + kv_specs(1) + kv_specs(2) + [
            ctx_spec(0), ctx_spec(1),
            pl.BlockSpec((2 * ATTN_PAIRS, N_SLAB_TYPES, 2 * GRID_W, SLAB_ROWS * GRID_W), lambda b, hg, rb: (hg, 0, 0, 0)),
        ],
        out_specs=pl.BlockSpec((ATTN_PAIRS, tq, LANES), lambda b, hg, rb: (hg, b * nrb + rb, 0)),
        out_shape=jax.ShapeDtypeStruct((N_PAIRS, n_tok, LANES), bf16),
        compiler_params=_cparams(3),
        name="attn",
    )(qkv, *([flat] * (2 * ATTN_PAIRS)), ckv, ckv, sb)


A_CHUNK = 8


def _filter_kernel(seq, w1t_ref, w1c_ref, w1s_ref, b1_ref, fr1_ref, w2t_ref, b2_ref, fr2_ref, w3t_ref, dl_ref,
                   fq_ref, o_ref):
    st = pl.program_id(0)
    npos = A_CHUNK * LANES
    n = st * npos + lax.broadcasted_iota(i32, (1, npos), 1)
    idx = jnp.where(n < seq, n, 2 * seq - n).astype(f32)
    t = idx / (seq - 1)
    ang = (2.0 * math.pi / seq) * idx
    fa = fq_ref[...] * ang
    z1 = w1t_ref[...] * t + _dot(w1c_ref[...], jnp.cos(fa), HI) + _dot(w1s_ref[...], -jnp.sin(fa), HI) + b1_ref[...]
    h1 = jnp.sin(fr1_ref[...] * z1)
    h2 = jnp.sin(fr2_ref[...] * (_dot(w2t_ref[...], h1, HI) + b2_ref[...]))
    filt = _dot(w3t_ref[0], h2, HI) * jnp.exp(-t * dl_ref[...])
    filt = jnp.where(n == seq, 0.0, filt)
    for al in range(A_CHUNK):
        o_ref[:, al, :] = filt[:, al * LANES:(al + 1) * LANES]


def _hyena_filter(seq, w1, b1, fr1, w2, b2, fr2, w3):
    na = 2 * seq // LANES
    steps = na // A_CHUNK
    fwd_steps = steps // 2
    col = lambda v: v.reshape(-1, 1).astype(f32)
    deltas = jnp.abs(jnp.linspace(math.log(DECAY_TARGET) / SLOW_DECAY_PCT, math.log(DECAY_TARGET) / FAST_DECAY_PCT,
                                  D_B, dtype=f32))
    freqs = jnp.linspace(1e-4, N_BANDS - 1, N_BANDS, dtype=f32)
    w3t = w3.astype(f32).T.reshape(2, D_B, FILT_HID)
    w1 = w1.astype(f32)
    args = (w1[0:1].T, w1[1:1 + N_BANDS].T, w1[1 + N_BANDS:].T, col(b1), col(fr1), w2.astype(f32).T, col(b2), col(fr2),
            w3t, col(deltas), col(freqs))
    full = lambda a: pl.BlockSpec(a.shape, lambda s: (0,) * a.ndim)
    in_specs = [full(a) for a in args]
    in_specs[8] = pl.BlockSpec((1, D_B, FILT_HID), lambda s: (s // fwd_steps, 0, 0))
    return pl.pallas_call(
        functools.partial(_filter_kernel, seq),
        grid=(steps,),
        in_specs=in_specs,
        out_specs=pl.BlockSpec((D_B, A_CHUNK, LANES), lambda s: (0, s, 0)),
        out_shape=jax.ShapeDtypeStruct((D_B, na, LANES), f32),
        compiler_params=_cparams(1),
        name="filt",
    )(*args)


def _dft_consts(seq):
    n = 2 * seq
    n1 = n // LANES
    a_used = seq // LANES
    k1 = np.arange(n1)[:, None]
    a = np.arange(n1)[None, :]
    ang1 = -2.0 * np.pi * ((k1 * a) % n1) / n1
    f1 = np.concatenate([np.cos(ang1), np.sin(ang1)], axis=0)
    b = np.arange(LANES)[None, :]
    angt = -2.0 * np.pi * ((k1 * b) % n) / n
    tw_r, tw_i = np.cos(angt), np.sin(angt)
    bb = np.arange(LANES)[:, None]
    k2 = np.arange(LANES)[None, :]
    angg = -2.0 * np.pi * ((bb * k2) % LANES) / LANES
    gr, gi = np.cos(angg), np.sin(angg)
    g_fwd = np.block([[gr, gi], [-gi, gr]])
    g_inv = np.block([[gr, -gi], [gi, gr]])
    f1_inv = np.concatenate([np.cos(ang1).T[:a_used], np.sin(ang1).T[:a_used]], axis=1)
    c = lambda x: jnp.asarray(x, f32)
    return dict(n1=n1, a_used=a_used, f1_full=c(f1), f1=c(f1[:, :a_used]), tw_r=c(tw_r), tw_i=c(tw_i), g_fwd=c(g_fwd),
                g_inv=c(g_inv), f1_inv=c(f1_inv))


def _pair_dot(m, x, precision=None):
    out = []
    for p in range(x.shape[0] // 2):
        y2 = _dot(m, jnp.concatenate([x[2 * p], x[2 * p + 1]], axis=1), precision)
        out += [y2[:, :LANES], y2[:, LANES:]]
    return jnp.stack(out)


def _fft_fwd(x, f1, tw_r, tw_i, g_fwd, precision, cast):
    db = x.shape[0]
    n1 = tw_r.shape[0]
    y = _pair_dot(cast(f1), cast(x), precision)
    yr, yi = y[:, :n1], y[:, n1:]
    ytr = yr * tw_r - yi * tw_i
    yti = yr * tw_i + yi * tw_r
    lhs = jnp.concatenate([ytr, yti], axis=-1).reshape(db * n1, 2 * LANES)
    return _dot(cast(lhs), cast(g_fwd), precision)


def _fftk_kernel(x_ref, f1_ref, twr_ref, twi_ref, g_ref, o_ref):
    db, n1 = x_ref.shape[0], twr_ref.shape[0]
    z = _fft_fwd(x_ref[...], f1_ref[...], twr_ref[...], twi_ref[...], g_ref[...], HI, lambda v: v)
    o_ref[...] = z.reshape(db, n1, 2 * LANES)


def _filter_spectrum(kf, dc):
    d, n1, _ = kf.shape
    db = 16
    full = lambda a: pl.BlockSpec(a.shape, lambda i: (0,) * a.ndim)
    consts = (dc["f1_full"], dc["tw_r"], dc["tw_i"], dc["g_fwd"])
    return pl.pallas_call(
        _fftk_kernel,
        grid=(d // db,),
        in_specs=[pl.BlockSpec((db, n1, LANES), lambda i: (i, 0, 0))] + [full(a) for a in consts],
        out_specs=pl.BlockSpec((db, n1, 2 * LANES), lambda i: (i, 0, 0)),
        out_shape=jax.ShapeDtypeStruct((d, n1, 2 * LANES), f32),
        compiler_params=_cparams(1),
        name="fftk",
    )(kf, *consts)


def _hpre_kernel(n_chunks, u_ref, up_ref, un_ref, cw_ref, cb_ref, zt_ref, x0_ref):
    ac = pl.program_id(1)
    tm = u_ref.shape[0]
    has_prev = (ac > 0).astype(f32)
    has_next = (ac < n_chunks - 1).astype(f32)
    row = lax.broadcasted_iota(i32, (tm, LANES), 0)
    halo = up_ref.shape[0]

    def conv(c0):
        sl = slice(c0, c0 + LANES)
        u = u_ref[:, sl].astype(f32)
        prev = up_ref[halo - 1:halo, sl].astype(f32) * has_prev
        nxt = un_ref[0:1, sl].astype(f32) * has_next
        um = jnp.where(row == 0, prev, pltpu.roll(u, 1, 0))
        upl = jnp.where(row == tm - 1, nxt, pltpu.roll(u, tm - 1, 0))
        return um * cw_ref[0:1, sl] + u * cw_ref[1:2, sl] + upl * cw_ref[2:3, sl] + cb_ref[:, sl]

    for dt in range(D_B // LANES):
        v = conv(dt * LANES)
        x1 = conv(D_B + dt * LANES)
        x0_ref[:, dt * LANES:(dt + 1) * LANES] = conv(2 * D_B + dt * LANES).astype(bf16)
        z = v * x1
        for al in range(A_CHUNK):
            zt_ref[0, dt * LANES:(dt + 1) * LANES, al, :] = z[al * LANES:(al + 1) * LANES, :].T


def _hyena_pre(u, conv_w, conv_b, batch, seq):
    n = u.shape[0]
    tm = A_CHUNK * LANES
    n_chunks = seq // tm
    halo = 16
    hb = tm // halo
    cw = jnp.concatenate([conv_w.astype(f32), jnp.zeros((8 - conv_w.shape[0], conv_w.shape[1]), f32)], axis=0)
    return pl.pallas_call(
        functools.partial(_hpre_kernel, n_chunks),
        grid=(batch, n_chunks),
        in_specs=[
            pl.BlockSpec((tm, 3 * D_B), lambda b, c: (b * n_chunks + c, 0)),
            pl.BlockSpec((halo, 3 * D_B), lambda b, c: (jnp.maximum((b * n_chunks + c) * hb - 1, 0), 0)),
            pl.BlockSpec((halo, 3 * D_B), lambda b, c: (jnp.minimum((b * n_chunks + c + 1) * hb, n // halo - 1), 0)),
            pl.BlockSpec((8, 3 * D_B), lambda b, c: (0, 0)),
            pl.BlockSpec((1, 3 * D_B), lambda b, c: (0, 0)),
        ],
        out_specs=[
            pl.BlockSpec((1, D_B, A_CHUNK, LANES), lambda b, c: (b, 0, c, 0)),
            pl.BlockSpec((tm, D_B), lambda b, c: (b * n_chunks + c, 0)),
        ],
        out_shape=[
            jax.ShapeDtypeStruct((batch, D_B, seq // LANES, LANES), f32),
            jax.ShapeDtypeStruct((n, D_B), bf16),
        ],
        compiler_params=_cparams(2),
        name="hpre",
    )(u, u, u, cw, conv_b.reshape(1, -1).astype(f32))


def _fftconv_kernel(z_ref, kh_ref, skip_ref, f1_ref, twr_ref, twi_ref, gf_ref, gi_ref, f1i_ref, y_ref):
    db, a_used = z_ref.shape[1], z_ref.shape[2]
    n1 = twr_ref.shape[0]
    tw_r, tw_i = twr_ref[...], twi_ref[...]
    cast = lambda v: v.astype(bf16)
    z = z_ref[0]
    spec = _fft_fwd(z, f1_ref[...], tw_r, tw_i, gf_ref[...], None, cast)
    kh = kh_ref[...].reshape(db * n1, 2 * LANES)
    sr, si = spec[:, :LANES], spec[:, LANES:]
    kr, ki = kh[:, :LANES], kh[:, LANES:]
    prod = jnp.concatenate([sr * kr - si * ki, sr * ki + si * kr], axis=-1)
    q = _dot(cast(prod), cast(gi_ref[...])).reshape(db, n1, 2 * LANES)
    qr, qi = q[:, :, :LANES], q[:, :, LANES:]
    rhs = jnp.concatenate([qr * tw_r + qi * tw_i, qi * tw_r - qr * tw_i], axis=1)
    y = _pair_dot(cast(f1i_ref[...]), cast(rhs)) * (1.0 / (n1 * LANES))
    y_ref[0] = y + z * skip_ref[...]


def _fftconv(zt, khat, skip, dc):
    batch, d, a_used, _ = zt.shape
    n1 = dc["n1"]
    db = 32
    consts = (dc["f1"], dc["tw_r"], dc["tw_i"], dc["g_fwd"], dc["g_inv"], dc["f1_inv"])
    full = lambda a: pl.BlockSpec(a.shape, lambda i, b: (0,) * a.ndim)
    return pl.pallas_call(
        _fftconv_kernel,
        grid=(d // db, batch),
        in_specs=[
            pl.BlockSpec((1, db, a_used, LANES), lambda i, b: (b, i, 0, 0)),
            pl.BlockSpec((db, n1, 2 * LANES), lambda i, b: (i, 0, 0)),
            pl.BlockSpec((db, 1, 1), lambda i, b: (i, 0, 0)),
        ] + [full(a) for a in consts],
        out_specs=pl.BlockSpec((1, db, a_used, LANES), lambda i, b: (b, i, 0, 0)),
        out_shape=jax.ShapeDtypeStruct(zt.shape, f32),
        compiler_params=_cparams(2),
        name="fftconv",
    )(zt, khat, skip.reshape(d, 1, 1).astype(f32), *consts)


def _hpost_kernel(yt_ref, x0_ref, ob_ref):
    for dt in range(D_B // LANES):
        for al in range(A_CHUNK):
            rows = slice(al * LANES, (al + 1) * LANES)
            cols = slice(dt * LANES, (dt + 1) * LANES)
            y = yt_ref[0, cols, al, :].T
            ob_ref[rows, cols] = (x0_ref[rows, cols].astype(f32) * y).astype(bf16)


def _hyena_post(yt, x0, batch, seq):
    tm = A_CHUNK * LANES
    n_chunks = seq // tm
    return pl.pallas_call(
        _hpost_kernel,
        grid=(batch, n_chunks),
        in_specs=[
            pl.BlockSpec((1, D_B, A_CHUNK, LANES), lambda b, c: (b, 0, c, 0)),
            pl.BlockSpec((tm, D_B), lambda b, c: (b * n_chunks + c, 0)),
        ],
        out_specs=pl.BlockSpec((tm, D_B), lambda b, c: (b * n_chunks + c, 0)),
        out_shape=jax.ShapeDtypeStruct(x0.shape, bf16),
        compiler_params=_cparams(2),
        name="hpost",
    )(yt, x0)


IDX_ROWS = 8


def _merge_kernel(oa_ref, ob_ref, ga_ref, gb_ref, x_ref, g1_ref, wba_ref, wbb_ref, wout_ref, n2g_ref, sh2_ref, sc2_ref,
                  wrt_ref, br_ref, x1_ref, h2_ref, idx_ref, wt_ref):
    oa = jnp.concatenate([oa_ref[hp] for hp in range(N_PAIRS)], axis=1)
    m = ga_ref[...].astype(f32) * _dot(oa, wba_ref[...]) + gb_ref[...].astype(f32) * _dot(ob_ref[...], wbb_ref[...])
    x1 = x_ref[...] + g1_ref[0] * _dot(m.astype(bf16), wout_ref[...])
    x1_ref[...] = x1
    ms = jnp.mean(x1 * x1, axis=-1, keepdims=True)
    h2 = (x1 * lax.rsqrt(ms + RMS_EPS) * n2g_ref[...]) * (1.0 + sc2_ref[0]) + sh2_ref[0]
    h2_ref[...] = h2.astype(bf16)
    logits = _dot_nt(wrt_ref[...], h2, HI) + br_ref[...]
    eio = lax.broadcasted_iota(i32, logits.shape, 0)
    vals = logits
    idxs, tops = [], []
    for _ in range(TOP_K):
        mx = jnp.max(vals, axis=0, keepdims=True)
        ix = jnp.min(jnp.where(vals == mx, eio, N_EXPERTS), axis=0, keepdims=True)
        idxs.append(ix)
        tops.append(mx)
        vals = jnp.where(eio == ix, -jnp.inf, vals)
    ex = [jnp.exp(v - tops[0]) for v in tops]
    den = ex[0] + ex[1] + ex[2] + ex[3]
    idx_ref[...] = jnp.concatenate(idxs + [jnp.full_like(idxs[0], -1)] * (IDX_ROWS - TOP_K), axis=0)
    wt_ref[...] = jnp.concatenate([e / den for e in ex] + [jnp.zeros_like(den)] * (IDX_ROWS - TOP_K), axis=0)


def _merge(oa3, ob, gates, x2, g1, wba, wbb, wout, n2g, sh2, sc2, wrt, br, tm, tiles_per_batch):
    n, d = x2.shape
    tok = lambda col=0: pl.BlockSpec((tm, d), lambda i: (i, col))
    mod = lambda: pl.BlockSpec((1, 1, d), lambda i: (i // tiles_per_batch, 0, 0))
    wsp = lambda: pl.BlockSpec((d, d), lambda i: (0, 0))
    ga, gb = gates, gates
    return pl.pallas_call(
        _merge_kernel,
        grid=(n // tm,),
        in_specs=[
            pl.BlockSpec((N_PAIRS, tm, LANES), lambda i: (0, i, 0)),
            tok(), tok(0), tok(1), tok(), mod(), wsp(), wsp(), wsp(),
            pl.BlockSpec((1, d), lambda i: (0, 0)), mod(), mod(),
            pl.BlockSpec((N_EXPERTS, d), lambda i: (0, 0)),
            pl.BlockSpec((N_EXPERTS, 1), lambda i: (0, 0)),
        ],
        out_specs=[tok(), tok(), pl.BlockSpec((IDX_ROWS, tm), lambda i: (0, i)), pl.BlockSpec((IDX_ROWS, tm), lambda i: (0, i))],
        out_shape=[
            jax.ShapeDtypeStruct((n, d), f32),
            jax.ShapeDtypeStruct((n, d), bf16),
            jax.ShapeDtypeStruct((IDX_ROWS, n), i32),
            jax.ShapeDtypeStruct((IDX_ROWS, n), f32),
        ],
        compiler_params=_cparams(1),
        name="merge",
    )(oa3, ob, ga, gb, x2, g1, wba, wbb, wout, n2g, sh2, sc2, wrt, br)


MOE_SUB = 1024
MOE_NSUB = 2
MOE_TILE = MOE_SUB * MOE_NSUB
MOE_ROWS = 160
MOE_CAP = 12 * MOE_ROWS


def _moe_kernel(h2_ref, idx_ref, wt_ref, wgu_ref, bgu_ref, wd_ref, bd_ref, x1_ref, g2_ref, y_ref, rank_scr, msk_scr,
                wte_scr, o_scr, p_scr, st_ref, cnt_ref):
    e = pl.program_id(1)
    ts = MOE_SUB
    last = e == N_EXPERTS - 1

    @pl.when(e == 0)
    def _():
        y_ref[...] = jnp.zeros_like(y_ref)
        o_scr[...] = jnp.zeros_like(o_scr)
        p_scr[...] = jnp.zeros_like(p_scr)
        eio = lax.broadcasted_iota(i32, (N_EXPERTS, ts), 0)
        lane = lax.broadcasted_iota(i32, (N_EXPERTS, ts), 1)

        for s in range(MOE_NSUB):
            idx = idx_ref[s]
            w = wt_ref[s]
            msk = jnp.zeros((N_EXPERTS, ts), f32)
            wte = jnp.zeros((N_EXPERTS, ts), f32)
            for k in range(TOP_K):
                hit = idx[k:k + 1, :] == eio
                msk = msk + hit.astype(f32)
                wte = wte + jnp.where(hit, w[k:k + 1, :], 0.0)
            csum = msk
            sh = 1
            while sh < ts:
                csum = csum + jnp.where(lane >= sh, pltpu.roll(csum, sh, 1), 0.0)
                sh *= 2
            rank_scr[s] = (csum - msk).astype(i32)
            msk_scr[s] = msk
            wte_scr[s] = wte
            st_ref[s] = 0
            for ex in range(N_EXPERTS):
                cnt_ref[s * N_EXPERTS + ex] = jnp.sum(csum[ex:ex + 1, ts - 1:ts]).astype(i32)

    subs = range(MOE_NSUB)
    toks = [slice(s * ts, (s + 1) * ts) for s in subs]
    ranks = [rank_scr[s, pl.ds(e, 1), :] for s in subs]
    sels = [msk_scr[s, pl.ds(e, 1), :] > 0.0 for s in subs]
    wtes = [wte_scr[s, pl.ds(e, 1), :] for s in subs]
    n_chunks = [(cnt_ref[s * N_EXPERTS + e] + MOE_ROWS - 1) // MOE_ROWS for s in subs]
    max_chunks = functools.reduce(jnp.maximum, n_chunks)

    def combine(s):
        y_ref[toks[s], :] += _dot_tn(p_scr[s], o_scr[s])
        o_scr[s] = jnp.zeros((MOE_CAP, D_MODEL), bf16)
        st_ref[s] = 0

    def trip(jc, carry):
        live = [jc < n for n in n_chunks]
        for s in subs:
            pl.when((live[s] & (st_ref[s] + MOE_ROWS > MOE_CAP)) | (jc >= max_chunks))(functools.partial(combine, s))

        @pl.when(jc < max_chunks)
        def _():
            jio = lax.broadcasted_iota(i32, (MOE_ROWS, ts), 0)
            hits = [((ranks[s] - jc * MOE_ROWS) == jio) & sels[s] for s in subs]
            onehots = [h.astype(bf16) for h in hits]
            xg = jnp.concatenate([_dot(onehots[s], h2_ref[toks[s], :]).astype(bf16) for s in subs], axis=0)
            gu = _dot(xg, wgu_ref[0]) + bgu_ref[0]
            gate = jnp.minimum(gu[:, :D_FF], SWIGLU_LIMIT)
            up = jnp.clip(gu[:, D_FF:], -SWIGLU_LIMIT, SWIGLU_LIMIT)
            act = (up + 1.0) * gate * _sigmoid(SWIGLU_ALPHA * gate)
            o = _dot(act.astype(bf16), wd_ref[0]) + bd_ref[0]
            for s in subs:

                @pl.when(live[s])
                def _(s=s):
                    fill = pl.multiple_of(st_ref[s], 16)
                    wrow = jnp.sum(jnp.where(hits[s], wtes[s], 0.0), axis=1, keepdims=True)
                    o_scr[s, pl.ds(fill, MOE_ROWS), :] = (o[s * MOE_ROWS:(s + 1) * MOE_ROWS] * wrow).astype(bf16)
                    p_scr[s, pl.ds(fill, MOE_ROWS), :] = onehots[s]
                    st_ref[s] = fill + MOE_ROWS

        return carry

    lax.fori_loop(0, max_chunks + last.astype(i32), trip, 0)

    @pl.when(last)
    def _():
        y_ref[...] = x1_ref[...] + g2_ref[0] * y_ref[...]


def _moe(h2, idx, wt, wgu, bgu, wd, bd, x1, g2, tiles_per_batch):
    n, d = h2.shape
    t = MOE_TILE
    once = dict(pipeline_mode=pl.Buffered(1))
    per_sub = lambda a: jnp.transpose(a.reshape(IDX_ROWS, n // MOE_SUB, MOE_SUB), (1, 0, 2))
    route_spec = lambda: pl.BlockSpec((MOE_NSUB, IDX_ROWS, MOE_SUB), lambda i, e: (i, 0, 0), **once)
    return pl.pallas_call(
        _moe_kernel,
        grid=(n // t, N_EXPERTS),
        in_specs=[
            pl.BlockSpec((t, d), lambda i, e: (i, 0), **once),
            route_spec(), route_spec(),
            pl.BlockSpec((1, d, 2 * D_FF), lambda i, e: (e, 0, 0)),
            pl.BlockSpec((1, 1, 2 * D_FF), lambda i, e: (e, 0, 0)),
            pl.BlockSpec((1, D_FF, d), lambda i, e: (e, 0, 0)),
            pl.BlockSpec((1, 1, d), lambda i, e: (e, 0, 0)),
            pl.BlockSpec((t, d), lambda i, e: (i, 0), **once),
            pl.BlockSpec((1, 1, d), lambda i, e: (i // tiles_per_batch, 0, 0)),
        ],
        out_specs=pl.BlockSpec((t, d), lambda i, e: (i, 0), **once),
        out_shape=jax.ShapeDtypeStruct((n, d), f32),
        scratch_shapes=[
            pltpu.VMEM((MOE_NSUB, N_EXPERTS, MOE_SUB), i32),
            pltpu.VMEM((MOE_NSUB, N_EXPERTS, MOE_SUB), f32),
            pltpu.VMEM((MOE_NSUB, N_EXPERTS, MOE_SUB), f32),
            pltpu.VMEM((MOE_NSUB, MOE_CAP, d), bf16),
            pltpu.VMEM((MOE_NSUB, MOE_CAP, MOE_SUB), bf16),
            pltpu.SMEM((MOE_NSUB,), i32),
            pltpu.SMEM((MOE_NSUB * N_EXPERTS,), i32),
        ],
        compiler_params=_cparams(2),
        name="moe",
    )(h2, per_sub(idx), per_sub(wt), wgu, bgu, wd, bd, x1, g2)


def _layer(x, c, ctx, c_ctx, w_ada, b_ada, norm1_g, norm2_g, w_in, q_norm_g, k_norm_g, rpb, conv_w, conv_b, filt_w1,
           filt_b1, filt_freq1, filt_w2, filt_b2, filt_freq2, filt_w3, hyena_skip, w_branch_a, w_branch_b, w_out,
           w_router, b_router, w_gate_up, b_gate_up, w_down, b_down):
    batch, seq, d = x.shape
    n = batch * seq
    n_ctx = ctx.shape[1]
    assert d == D_MODEL and n_ctx == CTX_LEN and seq % (A_CHUNK * LANES) == 0 and seq % MOE_TILE == 0

    rows = 8 * ((batch + 1 + 7) // 8)
    cv = jnp.zeros((rows, d), f32).at[:batch].set(c).at[batch].set(c_ctx)
    mod = _adaln(cv, w_ada, b_ada.reshape(1, -1))
    sh1, sc1, g1, sh2, sc2, g2 = [mod[:batch, i * d:(i + 1) * d].reshape(batch, 1, d) for i in range(N_MOD)]
    modc = jnp.broadcast_to(mod[batch:batch + 1], (batch, N_MOD * d))
    sh1c, sc1c = modc[:, :d].reshape(batch, 1, d), modc[:, d:2 * d].reshape(batch, 1, d)

    w_in_bf = w_in.astype(bf16)
    ones = jnp.asarray(np.kron(np.eye(2), np.ones((HEAD_DIM, HEAD_DIM))), bf16)
    qg = (jnp.tile(q_norm_g.astype(f32), N_HEADS) * (HEAD_DIM ** -0.5)).reshape(1, d)
    kg = jnp.tile(k_norm_g.astype(f32), N_HEADS).reshape(1, d)
    n1g = norm1_g.reshape(1, d)

    (ckv,) = _inproj(ctx.reshape(batch * n_ctx, d), n1g, sh1c, sc1c, w_in_bf, (kg,), ones, 2, 0, 0, 1, n_ctx, 1)
    tm = 512
    x2 = x.reshape(n, d)
    tm_in = 1024
    qkv, u, gates = _inproj(x2, n1g, sh1, sc1, w_in_bf, (qg, kg), ones, 3, 3, 2, 0, tm_in, seq // tm_in)

    oa3 = _attention(qkv, ckv, _slab_bias_tables(rpb), batch, seq)

    dc = _dft_consts(seq)
    kf = _hyena_filter(seq, filt_w1, filt_b1, filt_freq1, filt_w2, filt_b2, filt_freq2, filt_w3)
    khat = _filter_spectrum(kf, dc)
    zt, x0 = _hyena_pre(u, conv_w, conv_b, batch, seq)
    yt = _fftconv(zt, khat, hyena_skip, dc)
    ob = _hyena_post(yt, x0, batch, seq)

    x1, h2, idx, wt = _merge(oa3, ob, gates, x2, g1, w_branch_a.astype(bf16), w_branch_b.astype(bf16),
                             w_out.astype(bf16), norm2_g.reshape(1, d), sh2, sc2, w_router.astype(f32).T,
                             b_router.reshape(-1, 1).astype(f32), tm, seq // tm)
    out = _moe(h2, idx, wt, w_gate_up.astype(bf16), b_gate_up.reshape(N_EXPERTS, 1, -1).astype(f32),
               w_down.astype(bf16), b_down.reshape(N_EXPERTS, 1, -1).astype(f32), x1, g2, seq // MOE_TILE)
    return out.reshape(batch, seq, d)


def kernel(x, c, ctx, c_ctx, w_ada, b_ada, norm1_g, norm2_g, w_in, q_norm_g, k_norm_g, rpb, conv_w, conv_b, filt_w1, filt_b1, filt_freq1, filt_w2, filt_b2, filt_freq2, filt_w3, hyena_skip, w_branch_a, w_branch_b, w_out, w_router, b_router, w_gate_up, b_gate_up, w_down, b_down):
    assert w_ada.shape[0] == 1, "single-layer stack"
    return _layer(x, c, ctx, c_ctx, w_ada[0], b_ada[0], norm1_g[0], norm2_g[0], w_in[0], q_norm_g[0], k_norm_g[0],
                  rpb[0], conv_w[0], conv_b[0], filt_w1[0], filt_b1[0], filt_freq1[0], filt_w2[0], filt_b2[0],
                  filt_freq2[0], filt_w3[0], hyena_skip[0], w_branch_a[0], w_branch_b[0], w_out[0], w_router[0],
                  b_router[0], w_gate_up[0], b_gate_up[0], w_down[0], b_down[0])
```

```python
import functools
import math

import numpy as np
import jax
import jax.numpy as jnp
from jax import lax
from jax.experimental import pallas as pl
from jax.experimental.pallas import tpu as pltpu

f32, bf16, i32 = jnp.float32, jnp.bfloat16, jnp.int32
HI = lax.Precision.HIGHEST

LANES = 128
D_MODEL = 1024
N_HEADS = 16
HEAD_DIM = 64
N_PAIRS = N_HEADS // 2
GRID_W = 64
WIN_ROWS = 8
WIN_COLS = 16
CTX_LEN = 256
N_MOD = 6
RMS_EPS = 1e-6
D_B = 1024
EMB_DIM = 33
FILT_HID = 64
N_BANDS = (EMB_DIM - 1) // 2
DECAY_TARGET = 1e-2
FAST_DECAY_PCT = 0.3
SLOW_DECAY_PCT = 1.5
N_EXPERTS = 32
TOP_K = 4
D_FF = 1024
SWIGLU_LIMIT = 7.0
SWIGLU_ALPHA = 1.702
NEG_BIG = -1e30

QROWS = 8
KROWS = 16
SLAB_ROWS = 10
N_SLAB_TYPES = 5
VMEM_LIMIT = 56 * 1024 * 1024


def _cparams(ndim, vmem=VMEM_LIMIT):
    return pltpu.CompilerParams(dimension_semantics=("arbitrary",) * ndim, vmem_limit_bytes=vmem)


def _sigmoid(x):
    return 1.0 / (1.0 + jnp.exp(-x))


def _dot(a, b, precision=None):
    return jnp.dot(a, b, precision=precision, preferred_element_type=f32)


def _dot_nt(a, b, precision=None):
    return lax.dot_general(a, b, (((1,), (1,)), ((), ())), precision=precision, preferred_element_type=f32)


def _dot_tn(a, b):
    return lax.dot_general(a, b, (((0,), (0,)), ((), ())), preferred_element_type=f32)


def _bdot(a, b, precision=None):
    return lax.dot_general(a, b, (((2,), (1,)), ((0,), (0,))), precision=precision, preferred_element_type=f32)


def _adaln_kernel(cv_ref, w_ref, b_ref, o_ref):
    cv = cv_ref[...]
    o_ref[...] = _dot(cv * _sigmoid(cv), w_ref[...], HI) + b_ref[...]


def _adaln(cv, w, b):
    rows, d = cv.shape
    n = w.shape[1]
    tn = 1024
    return pl.pallas_call(
        _adaln_kernel,
        grid=(n // tn,),
        in_specs=[
            pl.BlockSpec((rows, d), lambda j: (0, 0)),
            pl.BlockSpec((d, tn), lambda j: (0, j)),
            pl.BlockSpec((1, tn), lambda j: (0, j)),
        ],
        out_specs=pl.BlockSpec((rows, tn), lambda j: (0, j)),
        out_shape=jax.ShapeDtypeStruct((rows, n), f32),
        compiler_params=_cparams(1),
        name="adaln",
    )(cv, w, b)


def _head_norm(acc, gain, ones):
    s2 = acc * acc
    hi = s2.astype(bf16)
    lo = (s2 - hi.astype(f32)).astype(bf16)
    ssum = _dot(hi, ones) + _dot(lo, ones)
    return acc * lax.rsqrt(ssum * (1.0 / HEAD_DIM) + RMS_EPS) * gain


def _inproj_kernel(n_norm, n_hm, n_u, n_g, x_ref, g_ref, sh_ref, sc_ref, w_ref, ones_ref, *rest):
    gains, rest = rest[:n_norm], list(rest[n_norm:])
    hm_ref = rest.pop(0)
    u_ref = rest.pop(0) if n_u else None
    gate_ref = rest.pop(0) if n_g else None
    h_scr = rest.pop(0)
    j = pl.program_id(1)

    @pl.when(j == 0)
    def _():
        xf = x_ref[...]
        ms = jnp.mean(xf * xf, axis=-1, keepdims=True)
        y = xf * lax.rsqrt(ms + RMS_EPS) * g_ref[...]
        h_scr[...] = (y * (1.0 + sc_ref[0]) + sh_ref[0]).astype(bf16)

    project = lambda: _dot(h_scr[...], w_ref[...])

    @pl.when(j < n_norm)
    def _():
        acc = project()
        gain = gains[n_norm - 1][...]
        for t in range(n_norm - 2, -1, -1):
            gain = jnp.where(j == t, gains[t][...], gain)
        for hp in range(N_PAIRS):
            sl = slice(hp * LANES, (hp + 1) * LANES)
            hm_ref[hp] = _head_norm(acc[:, sl], gain[:, sl], ones_ref[...]).astype(bf16)

    if n_hm > n_norm:

        @pl.when((j >= n_norm) & (j < n_hm))
        def _():
            acc = project()
            for hp in range(N_PAIRS):
                hm_ref[hp] = acc[:, hp * LANES:(hp + 1) * LANES].astype(bf16)

    if n_u:

        @pl.when((j >= n_hm) & (j < n_hm + n_u))
        def _():
            u_ref[...] = project().astype(bf16)

    if n_g:

        @pl.when(j >= n_hm + n_u)
        def _():
            gate_ref[...] = _sigmoid(project()).astype(bf16)


def _inproj(x2, gain, shift, scale, w_bf, norm_gains, ones, n_hm, n_u, n_g, col0, tm, tiles_per_batch):
    n, d = x2.shape
    n_norm = len(norm_gains)
    out_shapes = [jax.ShapeDtypeStruct((n_hm * N_PAIRS, n, LANES), bf16)]
    out_specs = [pl.BlockSpec((N_PAIRS, tm, LANES), lambda i, j: (jnp.clip(j, 0, n_hm - 1), i, 0))]
    for j0, ncol in ((n_hm, n_u), (n_hm + n_u, n_g)):
        if ncol:
            out_shapes.append(jax.ShapeDtypeStruct((n, d * ncol), bf16))
            out_specs.append(pl.BlockSpec((tm, d), lambda i, j, j0=j0, ncol=ncol: (i, jnp.clip(j - j0, 0, ncol - 1))))
    vec = lambda: pl.BlockSpec((1, d), lambda i, j: (0, 0))
    mod = lambda: pl.BlockSpec((1, 1, d), lambda i, j: (i // tiles_per_batch, 0, 0))
    return pl.pallas_call(
        functools.partial(_inproj_kernel, n_norm, n_hm, n_u, n_g),
        grid=(n // tm, n_hm + n_u + n_g),
        in_specs=[
            pl.BlockSpec((tm, d), lambda i, j: (i, 0)),
            vec(), mod(), mod(),
            pl.BlockSpec((d, d), lambda i, j: (0, j + col0)),
            pl.BlockSpec((LANES, LANES), lambda i, j: (0, 0)),
        ] + [vec() for _ in norm_gains],
        out_specs=out_specs,
        out_shape=out_shapes,
        scratch_shapes=[pltpu.VMEM((tm, d), bf16)],
        compiler_params=_cparams(2),
        name="inproj",
    )(x2, gain, shift, scale, w_bf, ones, *norm_gains)


def _slab_bias_tables(rpb):
    n_drow = 2 * WIN_ROWS - 1
    c = np.arange(GRID_W)
    cs = np.clip(c - WIN_COLS // 2, 0, GRID_W - WIN_COLS)
    j = np.arange(GRID_W)
    col_ok = (j[None, :] >= cs[:, None]) & (j[None, :] < cs[:, None] + WIN_COLS)
    pad = GRID_W - WIN_COLS
    rp = jnp.pad(rpb.astype(f32), ((0, 0), (0, 0), (pad, pad)))
    bm = jnp.stack([rp[:, :, pad + WIN_COLS - 1 - ci:pad + WIN_COLS - 1 - ci + GRID_W] for ci in range(GRID_W)], axis=2)
    bm = jnp.where(col_ok[None, None], bm, NEG_BIG)
    return pl.pallas_call(
        functools.partial(_slab_bias_kernel, n_drow),
        grid=(N_HEADS,),
        in_specs=[pl.BlockSpec((1, n_drow, GRID_W, GRID_W), lambda h: (h, 0, 0, 0))],
        out_specs=pl.BlockSpec((1, N_SLAB_TYPES, 2 * GRID_W, SLAB_ROWS * GRID_W), lambda h: (h, 0, 0, 0)),
        out_shape=jax.ShapeDtypeStruct((N_HEADS, N_SLAB_TYPES, 2 * GRID_W, SLAB_ROWS * GRID_W), f32),
        compiler_params=_cparams(1),
        name="slabbias",
    )(bm)


_SLAB_SPEC = {0: [(3, 0, 8), (2, 1, 9)], 1: [(7, 0, 8), (6, 0, 8)], 2: [(5, 0, 8), (4, 0, 8)],
              3: [(1, 2, 10), (0, 2, 10)], 4: [(-1, 2, 10), (-2, 2, 10)]}


def _slab_bias_kernel(n_drow, bm_ref, o_ref):
    masked = jnp.full((GRID_W, GRID_W), NEG_BIG, f32)
    for t in range(N_SLAB_TYPES):
        for qi, (start, lo, hi) in enumerate(_SLAB_SPEC[t]):
            for kl in range(SLAB_ROWS):
                drow = start + kl
                ok = lo <= kl < hi and 0 <= drow < n_drow
                tile = bm_ref[0, drow] if ok else masked
                o_ref[0, t, qi * GRID_W:(qi + 1) * GRID_W, kl * GRID_W:(kl + 1) * GRID_W] = tile


ATTN_PAIRS = 2


def _attn_kernel(n_rows, q_ref, *refs):
    k_refs, v_refs = refs[:ATTN_PAIRS], refs[ATTN_PAIRS:2 * ATTN_PAIRS]
    kc_ref, vc_ref, sb_ref, o_ref = refs[2 * ATTN_PAIRS:]
    for g in range(ATTN_PAIRS):
        _attn_head_pair(n_rows, g, q_ref, k_refs[g], v_refs[g], kc_ref, vc_ref, sb_ref, o_ref)


def _attn_head_pair(n_rows, g, q_ref, k_ref, v_ref, kc_ref, vc_ref, sb_ref, o_ref):
    rb = pl.program_id(2)
    base = jnp.clip(QROWS * rb - WIN_ROWS // 2, 0, n_rows - KROWS)
    tq = q_ref.shape[1]
    pr = 2 * GRID_W
    slab = SLAB_ROWS * GRID_W
    lane = lax.broadcasted_iota(i32, (tq, LANES), 1)
    q = q_ref[g]
    zero = jnp.zeros_like(q)
    q2 = jnp.concatenate([jnp.where(lane < HEAD_DIM, q, zero), jnp.where(lane >= HEAD_DIM, q, zero)], axis=0)
    sc_all = _dot_nt(q2, kc_ref[g])
    pws, pcs, dens = [], [], []
    for ip in range(QROWS // 2):
        r0 = QROWS * rb + 2 * ip
        rs0 = jnp.clip(r0 - WIN_ROWS // 2, 0, n_rows - WIN_ROWS)
        s = jnp.minimum(rs0 - base, KROWS - SLAB_ROWS)
        typ = jnp.where(r0 < 2, 1, jnp.where(r0 < 4, 2, jnp.where(r0 == n_rows - 4, 3, jnp.where(r0 == n_rows - 2, 4, 0))))
        off = pl.multiple_of(s * GRID_W, GRID_W)
        kw = k_ref[0, pl.ds(off, slab), :]
        ra, rbb = slice(ip * pr, (ip + 1) * pr), slice(tq + ip * pr, tq + (ip + 1) * pr)
        qq = jnp.concatenate([q2[ra], q2[rbb]], axis=0)
        sw = _dot_nt(qq, kw) + jnp.concatenate([sb_ref[2 * g, typ], sb_ref[2 * g + 1, typ]], axis=0)
        sc = jnp.concatenate([sc_all[ra], sc_all[rbb]], axis=0)
        m = jnp.maximum(jnp.max(sw, axis=-1, keepdims=True), jnp.max(sc, axis=-1, keepdims=True))
        pw = jnp.exp(sw - m)
        pc = jnp.exp(sc - m)
        dens.append(jnp.sum(pw, axis=-1, keepdims=True) + jnp.sum(pc, axis=-1, keepdims=True))
        pws.append((pw.astype(bf16), off))
        pcs.append(pc.astype(bf16))
    oc = _dot(jnp.concatenate(pcs, axis=0), vc_ref[g])
    lane_p = lax.broadcasted_iota(i32, (pr, LANES), 1)
    for ip in range(QROWS // 2):
        pw, off = pws[ip]
        o = (_dot(pw, v_ref[0, pl.ds(off, slab), :]) + oc[ip * 2 * pr:(ip + 1) * 2 * pr]) / dens[ip]
        o_ref[g, ip * pr:(ip + 1) * pr, :] = jnp.where(lane_p < HEAD_DIM, o[:pr], o[pr:]).astype(bf16)


def _attention(qkv, ckv, sb, batch, seq):
    n_rows = seq // GRID_W
    assert n_rows % QROWS == 0 and n_rows >= KROWS
    nrb = n_rows // QROWS
    tq = QROWS * GRID_W
    tk = KROWS * GRID_W
    n_tok = batch * seq

    def kv_map(which, g):
        def index(b, hg, rb):
            base = jnp.clip(QROWS * rb - WIN_ROWS // 2, 0, n_rows - KROWS)
            return (which * N_PAIRS + hg * ATTN_PAIRS + g, (b * n_rows + base) * GRID_W, 0)
        return index

    kv_specs = lambda which: [pl.BlockSpec((pl.Element(1), pl.Element(tk), pl.Element(LANES)), kv_map(which, g))
                              for g in range(ATTN_PAIRS)]
    n_groups = N_PAIRS // ATTN_PAIRS
    ctx_spec = lambda which: pl.BlockSpec((ATTN_PAIRS, CTX_LEN, LANES),
                                          lambda b, hg, rb: (which * n_groups + hg, b, 0))
    return pl.pallas_call(
        functools.partial(_attn_kernel, n_rows),
        grid=(batch, n_groups, nrb),
        in_specs=[pl.BlockSpec((ATTN_PAIRS, tq, LANES), lambda b, hg, rb: (hg, b * nrb + rb, 0))]
        + kv_specs(1) + kv_specs(2) + [
            ctx_spec(0), ctx_spec(1),
            pl.BlockSpec((2 * ATTN_PAIRS, N_SLAB_TYPES, 2 * GRID_W, SLAB_ROWS * GRID_W), lambda b, hg, rb: (hg, 0, 0, 0)),
        ],
        out_specs=pl.BlockSpec((ATTN_PAIRS, tq, LANES), lambda b, hg, rb: (hg, b * nrb + rb, 0)),
        out_shape=jax.ShapeDtypeStruct((N_PAIRS, n_tok, LANES), bf16),
        compiler_params=_cparams(3),
        name="attn",
    )(qkv, *([qkv] * (2 * ATTN_PAIRS)), ckv, ckv, sb)


A_CHUNK = 8


def _filter_kernel(seq, w1t_ref, w1c_ref, w1s_ref, b1_ref, fr1_ref, w2t_ref, b2_ref, fr2_ref, w3t_ref, dl_ref,
                   fq_ref, o_ref):
    st = pl.program_id(0)
    npos = A_CHUNK * LANES
    n = st * npos + lax.broadcasted_iota(i32, (1, npos), 1)
    idx = jnp.where(n < seq, n, 2 * seq - n).astype(f32)
    t = idx / (seq - 1)
    ang = (2.0 * math.pi / seq) * idx
    fa = fq_ref[...] * ang
    z1 = w1t_ref[...] * t + _dot(w1c_ref[...], jnp.cos(fa), HI) + _dot(w1s_ref[...], -jnp.sin(fa), HI) + b1_ref[...]
    h1 = jnp.sin(fr1_ref[...] * z1)
    h2 = jnp.sin(fr2_ref[...] * (_dot(w2t_ref[...], h1, HI) + b2_ref[...]))
    filt = _dot(w3t_ref[0], h2, HI) * jnp.exp(-t * dl_ref[...])
    filt = jnp.where(n == seq, 0.0, filt)
    for al in range(A_CHUNK):
        o_ref[0, pl.ds(al, D_B, stride=A_CHUNK), :] = filt[:, al * LANES:(al + 1) * LANES]


def _from_chunks(blk, db):
    cat = lambda d: jnp.concatenate([blk[c, d * A_CHUNK:(d + 1) * A_CHUNK] for c in range(blk.shape[0])], axis=0)
    return jnp.stack([cat(d) for d in range(db)])


def _to_chunks(y):
    cat = lambda c: jnp.concatenate([y[d, c * A_CHUNK:(c + 1) * A_CHUNK] for d in range(y.shape[0])], axis=0)
    return jnp.stack([cat(c) for c in range(y.shape[1] // A_CHUNK)])


def _hyena_filter(seq, w1, b1, fr1, w2, b2, fr2, w3):
    na = 2 * seq // LANES
    steps = na // A_CHUNK
    fwd_steps = steps // 2
    col = lambda v: v.reshape(-1, 1).astype(f32)
    deltas = jnp.abs(jnp.linspace(math.log(DECAY_TARGET) / SLOW_DECAY_PCT, math.log(DECAY_TARGET) / FAST_DECAY_PCT,
                                  D_B, dtype=f32))
    freqs = jnp.linspace(1e-4, N_BANDS - 1, N_BANDS, dtype=f32)
    w3t = w3.astype(f32).T.reshape(2, D_B, FILT_HID)
    w1 = w1.astype(f32)
    args = (w1[0:1].T, w1[1:1 + N_BANDS].T, w1[1 + N_BANDS:].T, col(b1), col(fr1), w2.astype(f32).T, col(b2), col(fr2),
            w3t, col(deltas), col(freqs))
    full = lambda a: pl.BlockSpec(a.shape, lambda s: (0,) * a.ndim)
    in_specs = [full(a) for a in args]
    in_specs[8] = pl.BlockSpec((1, D_B, FILT_HID), lambda s: (s // fwd_steps, 0, 0))
    return pl.pallas_call(
        functools.partial(_filter_kernel, seq),
        grid=(steps,),
        in_specs=in_specs,
        out_specs=pl.BlockSpec((1, D_B * A_CHUNK, LANES), lambda s: (s, 0, 0)),
        out_shape=jax.ShapeDtypeStruct((steps, D_B * A_CHUNK, LANES), f32),
        compiler_params=_cparams(1),
        name="filt",
    )(*args)


def _dft_consts(seq):
    n = 2 * seq
    n1 = n // LANES
    a_used = seq // LANES
    k1 = np.arange(n1)[:, None]
    a = np.arange(n1)[None, :]
    ang1 = -2.0 * np.pi * ((k1 * a) % n1) / n1
    f1 = np.concatenate([np.cos(ang1), np.sin(ang1)], axis=0)
    b = np.arange(LANES)[None, :]
    angt = -2.0 * np.pi * ((k1 * b) % n) / n
    tw_r, tw_i = np.cos(angt), np.sin(angt)
    bb = np.arange(LANES)[:, None]
    k2 = np.arange(LANES)[None, :]
    angg = -2.0 * np.pi * ((bb * k2) % LANES) / LANES
    gr, gi = np.cos(angg), np.sin(angg)
    g_fwd = np.block([[gr, gi], [-gi, gr]])
    g_inv = np.block([[gr, -gi], [gi, gr]])
    f1_inv = np.concatenate([np.cos(ang1).T[:a_used], np.sin(ang1).T[:a_used]], axis=1)
    c = lambda x: jnp.asarray(x, f32)
    return dict(n1=n1, a_used=a_used, f1_full=c(f1), f1=c(f1[:, :a_used]), tw_r=c(tw_r), tw_i=c(tw_i), g_fwd=c(g_fwd),
                g_inv=c(g_inv), f1_inv=c(f1_inv))


def _pair_dot(m, x, precision=None):
    out = []
    for p in range(x.shape[0] // 2):
        y2 = _dot(m, jnp.concatenate([x[2 * p], x[2 * p + 1]], axis=1), precision)
        out += [y2[:, :LANES], y2[:, LANES:]]
    return jnp.stack(out)


def _fft_fwd(x, f1, tw_r, tw_i, g_fwd, precision, cast):
    db = x.shape[0]
    n1 = tw_r.shape[0]
    y = _pair_dot(cast(f1), cast(x), precision)
    yr, yi = y[:, :n1], y[:, n1:]
    ytr = yr * tw_r - yi * tw_i
    yti = yr * tw_i + yi * tw_r
    lhs = jnp.concatenate([ytr, yti], axis=-1).reshape(db * n1, 2 * LANES)
    return _dot(cast(lhs), cast(g_fwd), precision)


def _fftk_kernel(x_ref, f1_ref, twr_ref, twi_ref, g_ref, o_ref):
    db, n1 = o_ref.shape[0], twr_ref.shape[0]
    x = _from_chunks(x_ref[...], db)
    z = _fft_fwd(x, f1_ref[...], twr_ref[...], twi_ref[...], g_ref[...], HI, lambda v: v)
    o_ref[...] = z.reshape(db, n1, 2 * LANES)


def _filter_spectrum(kf, dc):
    n_chunks = kf.shape[0]
    d, n1 = kf.shape[1] // A_CHUNK, n_chunks * A_CHUNK
    db = 16
    full = lambda a: pl.BlockSpec(a.shape, lambda i: (0,) * a.ndim)
    consts = (dc["f1_full"], dc["tw_r"], dc["tw_i"], dc["g_fwd"])
    return pl.pallas_call(
        _fftk_kernel,
        grid=(d // db,),
        in_specs=[pl.BlockSpec((n_chunks, db * A_CHUNK, LANES), lambda i: (0, i, 0))] + [full(a) for a in consts],
        out_specs=pl.BlockSpec((db, n1, 2 * LANES), lambda i: (i, 0, 0)),
        out_shape=jax.ShapeDtypeStruct((d, n1, 2 * LANES), f32),
        compiler_params=_cparams(1),
        name="fftk",
    )(kf, *consts)


def _hpre_kernel(n_chunks, u_ref, up_ref, un_ref, cw_ref, cb_ref, zt_ref, x0_ref):
    ac = pl.program_id(1)
    tm = u_ref.shape[0]
    has_prev = (ac > 0).astype(f32)
    has_next = (ac < n_chunks - 1).astype(f32)
    row = lax.broadcasted_iota(i32, (tm, LANES), 0)
    halo = up_ref.shape[0]

    def conv(c0):
        sl = slice(c0, c0 + LANES)
        u = u_ref[:, sl].astype(f32)
        prev = up_ref[halo - 1:halo, sl].astype(f32) * has_prev
        nxt = un_ref[0:1, sl].astype(f32) * has_next
        um = jnp.where(row == 0, prev, pltpu.roll(u, 1, 0))
        upl = jnp.where(row == tm - 1, nxt, pltpu.roll(u, tm - 1, 0))
        return um * cw_ref[0:1, sl] + u * cw_ref[1:2, sl] + upl * cw_ref[2:3, sl] + cb_ref[:, sl]

    for dt in range(D_B // LANES):
        v = conv(dt * LANES)
        x1 = conv(D_B + dt * LANES)
        x0_ref[:, dt * LANES:(dt + 1) * LANES] = conv(2 * D_B + dt * LANES).astype(bf16)
        z = v * x1
        for al in range(A_CHUNK):
            rows = pl.ds(dt * LANES * A_CHUNK + al, LANES, stride=A_CHUNK)
            zt_ref[0, 0, rows, :] = z[al * LANES:(al + 1) * LANES, :].T


def _hyena_pre(u, conv_w, conv_b, batch, seq):
    n = u.shape[0]
    tm = A_CHUNK * LANES
    n_chunks = seq // tm
    halo = 16
    hb = tm // halo
    cw = jnp.concatenate([conv_w.astype(f32), jnp.zeros((8 - conv_w.shape[0], conv_w.shape[1]), f32)], axis=0)
    return pl.pallas_call(
        functools.partial(_hpre_kernel, n_chunks),
        grid=(batch, n_chunks),
        in_specs=[
            pl.BlockSpec((tm, 3 * D_B), lambda b, c: (b * n_chunks + c, 0)),
            pl.BlockSpec((halo, 3 * D_B), lambda b, c: (jnp.maximum((b * n_chunks + c) * hb - 1, 0), 0)),
            pl.BlockSpec((halo, 3 * D_B), lambda b, c: (jnp.minimum((b * n_chunks + c + 1) * hb, n // halo - 1), 0)),
            pl.BlockSpec((8, 3 * D_B), lambda b, c: (0, 0)),
            pl.BlockSpec((1, 3 * D_B), lambda b, c: (0, 0)),
        ],
        out_specs=[
            pl.BlockSpec((1, 1, D_B * A_CHUNK, LANES), lambda b, c: (b, c, 0, 0)),
            pl.BlockSpec((tm, D_B), lambda b, c: (b * n_chunks + c, 0)),
        ],
        out_shape=[
            jax.ShapeDtypeStruct((batch, n_chunks, D_B * A_CHUNK, LANES), f32),
            jax.ShapeDtypeStruct((n, D_B), bf16),
        ],
        compiler_params=_cparams(2),
        name="hpre",
    )(u, u, u, cw, conv_b.reshape(1, -1).astype(f32))


def _fftconv_kernel(z_ref, kh_ref, skip_ref, f1_ref, twr_ref, twi_ref, gf_ref, gi_ref, f1i_ref, y_ref):
    db = skip_ref.shape[0]
    n1 = twr_ref.shape[0]
    tw_r, tw_i = twr_ref[...], twi_ref[...]
    cast = lambda v: v.astype(bf16)
    z = _from_chunks(z_ref[0], db)
    spec = _fft_fwd(z, f1_ref[...], tw_r, tw_i, gf_ref[...], None, cast)
    kh = kh_ref[...].reshape(db * n1, 2 * LANES)
    sr, si = spec[:, :LANES], spec[:, LANES:]
    kr, ki = kh[:, :LANES], kh[:, LANES:]
    prod = jnp.concatenate([sr * kr - si * ki, sr * ki + si * kr], axis=-1)
    q = _dot(cast(prod), cast(gi_ref[...])).reshape(db, n1, 2 * LANES)
    qr, qi = q[:, :, :LANES], q[:, :, LANES:]
    rhs = jnp.concatenate([qr * tw_r + qi * tw_i, qi * tw_r - qr * tw_i], axis=1)
    y = _pair_dot(cast(f1i_ref[...]), cast(rhs)) * (1.0 / (n1 * LANES))
    y_ref[0] = _to_chunks(y + z * skip_ref[...])


def _fftconv(zt, khat, skip, dc):
    batch, n_chunks, rows, _ = zt.shape
    d = rows // A_CHUNK
    n1 = dc["n1"]
    db = 32
    consts = (dc["f1"], dc["tw_r"], dc["tw_i"], dc["g_fwd"], dc["g_inv"], dc["f1_inv"])
    full = lambda a: pl.BlockSpec(a.shape, lambda i, b: (0,) * a.ndim)
    blk = lambda: pl.BlockSpec((1, n_chunks, db * A_CHUNK, LANES), lambda i, b: (b, 0, i, 0))
    return pl.pallas_call(
        _fftconv_kernel,
        grid=(d // db, batch),
        in_specs=[
            blk(),
            pl.BlockSpec((db, n1, 2 * LANES), lambda i, b: (i, 0, 0)),
            pl.BlockSpec((db, 1, 1), lambda i, b: (i, 0, 0)),
        ] + [full(a) for a in consts],
        out_specs=blk(),
        out_shape=jax.ShapeDtypeStruct(zt.shape, f32),
        compiler_params=_cparams(2),
        name="fftconv",
    )(zt, khat, skip.reshape(d, 1, 1).astype(f32), *consts)


IDX_ROWS = 8


def _merge_kernel(oa_ref, yc_ref, x0_ref, ga_ref, gb_ref, x_ref, g1_ref, wba_ref, wbb_ref, wout_ref, n2g_ref, sh2_ref,
                  sc2_ref, wrt_ref, br_ref, x1_ref, h2_ref, idx_ref, wt_ref, ob_scr):
    a_tile = x0_ref.shape[0] // LANES
    part = pl.program_id(0) % (A_CHUNK // a_tile)
    for dt in range(D_B // LANES):
        for al in range(a_tile):
            rows = slice(al * LANES, (al + 1) * LANES)
            cols = slice(dt * LANES, (dt + 1) * LANES)
            src = pl.ds(dt * LANES * A_CHUNK + part * a_tile + al, LANES, stride=A_CHUNK)
            ob_scr[rows, cols] = (x0_ref[rows, cols].astype(f32) * yc_ref[0, 0, src, :].T).astype(bf16)
    oa = jnp.concatenate([oa_ref[hp] for hp in range(N_PAIRS)], axis=1)
    m = ga_ref[...].astype(f32) * _dot(oa, wba_ref[...]) + gb_ref[...].astype(f32) * _dot(ob_scr[...], wbb_ref[...])
    x1 = x_ref[...] + g1_ref[0] * _dot(m.astype(bf16), wout_ref[...])
    x1_ref[...] = x1
    ms = jnp.mean(x1 * x1, axis=-1, keepdims=True)
    h2 = (x1 * lax.rsqrt(ms + RMS_EPS) * n2g_ref[...]) * (1.0 + sc2_ref[0]) + sh2_ref[0]
    h_hi = h2.astype(bf16)
    h2_ref[...] = h_hi
    h_lo = (h2 - h_hi.astype(f32)).astype(bf16)
    wr = wrt_ref[...]
    w_hi = wr.astype(bf16)
    w_lo = (wr - w_hi.astype(f32)).astype(bf16)
    logits = _dot_nt(w_hi, h_hi) + _dot_nt(w_hi, h_lo) + _dot_nt(w_lo, h_hi) + br_ref[...]
    eio = lax.broadcasted_iota(i32, logits.shape, 0)
    vals = logits
    idxs, tops = [], []
    for _ in range(TOP_K):
        mx = jnp.max(vals, axis=0, keepdims=True)
        ix = jnp.min(jnp.where(vals == mx, eio, N_EXPERTS), axis=0, keepdims=True)
        idxs.append(ix)
        tops.append(mx)
        vals = jnp.where(eio == ix, -jnp.inf, vals)
    ex = [jnp.exp(v - tops[0]) for v in tops]
    den = ex[0] + ex[1] + ex[2] + ex[3]
    idx_ref[...] = jnp.concatenate(idxs + [jnp.full_like(idxs[0], -1)] * (IDX_ROWS - TOP_K), axis=0)
    wt_ref[...] = jnp.concatenate([e / den for e in ex] + [jnp.zeros_like(den)] * (IDX_ROWS - TOP_K), axis=0)


def _merge(oa3, yc, x0, gates, x2, g1, wba, wbb, wout, n2g, sh2, sc2, wrt, br, tm, tiles_per_batch):
    n, d = x2.shape
    parts = A_CHUNK * LANES // tm
    tok = lambda col=0: pl.BlockSpec((tm, d), lambda i: (i, col))
    mod = lambda: pl.BlockSpec((1, 1, d), lambda i: (i // tiles_per_batch, 0, 0))
    wsp = lambda: pl.BlockSpec((d, d), lambda i: (0, 0))
    ga, gb = gates, gates
    return pl.pallas_call(
        _merge_kernel,
        grid=(n // tm,),
        in_specs=[
            pl.BlockSpec((N_PAIRS, tm, LANES), lambda i: (0, i, 0)),
            pl.BlockSpec((1, 1, D_B * A_CHUNK, LANES),
                         lambda i: (i // tiles_per_batch, (i % tiles_per_batch) // parts, 0, 0)),
            tok(), tok(0), tok(1), tok(), mod(), wsp(), wsp(), wsp(),
            pl.BlockSpec((1, d), lambda i: (0, 0)), mod(), mod(),
            pl.BlockSpec((N_EXPERTS, d), lambda i: (0, 0)),
            pl.BlockSpec((N_EXPERTS, 1), lambda i: (0, 0)),
        ],
        out_specs=[tok(), tok(), pl.BlockSpec((IDX_ROWS, tm), lambda i: (0, i)), pl.BlockSpec((IDX_ROWS, tm), lambda i: (0, i))],
        out_shape=[
            jax.ShapeDtypeStruct((n, d), f32),
            jax.ShapeDtypeStruct((n, d), bf16),
            jax.ShapeDtypeStruct((IDX_ROWS, n), i32),
            jax.ShapeDtypeStruct((IDX_ROWS, n), f32),
        ],
        scratch_shapes=[pltpu.VMEM((tm, d), bf16)],
        compiler_params=_cparams(1),
        name="merge",
    )(oa3, yc, x0, ga, gb, x2, g1, wba, wbb, wout, n2g, sh2, sc2, wrt, br)


MOE_SUB = 1024
MOE_NSUB = 2
MOE_TILE = MOE_SUB * MOE_NSUB
MOE_ROWS = 160
MOE_CAP = 12 * MOE_ROWS


def _moe_kernel(h2_ref, idx_ref, wt_ref, wgu_ref, bgu_ref, wd_ref, bd_ref, x1_ref, g2_ref, y_ref, rank_scr, msk_scr,
                wte_scr, o_scr, p_scr, st_ref, cnt_ref):
    e = pl.program_id(1)
    ts = MOE_SUB
    last = e == N_EXPERTS - 1

    @pl.when(e == 0)
    def _():
        y_ref[...] = jnp.zeros_like(y_ref)
        o_scr[...] = jnp.zeros_like(o_scr)
        p_scr[...] = jnp.zeros_like(p_scr)
        eio = lax.broadcasted_iota(i32, (N_EXPERTS, ts), 0)
        lane = lax.broadcasted_iota(i32, (N_EXPERTS, ts), 1)

        for s in range(MOE_NSUB):
            idx = idx_ref[s]
            w = wt_ref[s]
            msk = jnp.zeros((N_EXPERTS, ts), f32)
            wte = jnp.zeros((N_EXPERTS, ts), f32)
            for k in range(TOP_K):
                hit = idx[k:k + 1, :] == eio
                msk = msk + hit.astype(f32)
                wte = wte + jnp.where(hit, w[k:k + 1, :], 0.0)
            csum = msk
            sh = 1
            while sh < ts:
                csum = csum + jnp.where(lane >= sh, pltpu.roll(csum, sh, 1), 0.0)
                sh *= 2
            rank_scr[s] = (csum - msk).astype(i32)
            msk_scr[s] = msk
            wte_scr[s] = wte
            st_ref[s] = 0
            for ex in range(N_EXPERTS):
                cnt_ref[s * N_EXPERTS + ex] = jnp.sum(csum[ex:ex + 1, ts - 1:ts]).astype(i32)

    subs = range(MOE_NSUB)
    toks = [slice(s * ts, (s + 1) * ts) for s in subs]
    ranks = [rank_scr[s, pl.ds(e, 1), :] for s in subs]
    sels = [msk_scr[s, pl.ds(e, 1), :] > 0.0 for s in subs]
    wtes = [wte_scr[s, pl.ds(e, 1), :] for s in subs]
    n_chunks = [(cnt_ref[s * N_EXPERTS + e] + MOE_ROWS - 1) // MOE_ROWS for s in subs]
    max_chunks = functools.reduce(jnp.maximum, n_chunks)

    def combine(s):
        y_ref[toks[s], :] += _dot_tn(p_scr[s], o_scr[s])
        o_scr[s] = jnp.zeros((MOE_CAP, D_MODEL), bf16)
        st_ref[s] = 0

    def trip(jc, carry):
        live = [jc < n for n in n_chunks]
        for s in subs:
            pl.when((live[s] & (st_ref[s] + MOE_ROWS > MOE_CAP)) | (jc >= max_chunks))(functools.partial(combine, s))

        @pl.when(jc < max_chunks)
        def _():
            jio = lax.broadcasted_iota(i32, (MOE_ROWS, ts), 0)
            hits = [((ranks[s] - jc * MOE_ROWS) == jio) & sels[s] for s in subs]
            onehots = [h.astype(bf16) for h in hits]
            xg = jnp.concatenate([_dot(onehots[s], h2_ref[toks[s], :]).astype(bf16) for s in subs], axis=0)
            gu = _dot(xg, wgu_ref[0]) + bgu_ref[0]
            gate = jnp.minimum(gu[:, :D_FF], SWIGLU_LIMIT)
            up = jnp.clip(gu[:, D_FF:], -SWIGLU_LIMIT, SWIGLU_LIMIT)
            act = (up + 1.0) * gate * _sigmoid(SWIGLU_ALPHA * gate)
            o = _dot(act.astype(bf16), wd_ref[0]) + bd_ref[0]
            for s in subs:

                @pl.when(live[s])
                def _(s=s):
                    fill = pl.multiple_of(st_ref[s], 16)
                    wrow = jnp.sum(jnp.where(hits[s], wtes[s], 0.0), axis=1, keepdims=True)
                    o_scr[s, pl.ds(fill, MOE_ROWS), :] = (o[s * MOE_ROWS:(s + 1) * MOE_ROWS] * wrow).astype(bf16)
                    p_scr[s, pl.ds(fill, MOE_ROWS), :] = onehots[s]
                    st_ref[s] = fill + MOE_ROWS

        return carry

    lax.fori_loop(0, max_chunks + last.astype(i32), trip, 0)

    @pl.when(last)
    def _():
        y_ref[...] = x1_ref[...] + g2_ref[0] * y_ref[...]


def _moe(h2, idx, wt, wgu, bgu, wd, bd, x1, g2, tiles_per_batch):
    n, d = h2.shape
    t = MOE_TILE
    once = dict(pipeline_mode=pl.Buffered(1))
    per_sub = lambda a: jnp.transpose(a.reshape(IDX_ROWS, n // MOE_SUB, MOE_SUB), (1, 0, 2))
    route_spec = lambda: pl.BlockSpec((MOE_NSUB, IDX_ROWS, MOE_SUB), lambda i, e: (i, 0, 0), **once)
    return pl.pallas_call(
        _moe_kernel,
        grid=(n // t, N_EXPERTS),
        in_specs=[
            pl.BlockSpec((t, d), lambda i, e: (i, 0), **once),
            route_spec(), route_spec(),
            pl.BlockSpec((1, d, 2 * D_FF), lambda i, e: (e, 0, 0)),
            pl.BlockSpec((1, 1, 2 * D_FF), lambda i, e: (e, 0, 0)),
            pl.BlockSpec((1, D_FF, d), lambda i, e: (e, 0, 0)),
            pl.BlockSpec((1, 1, d), lambda i, e: (e, 0, 0)),
            pl.BlockSpec((t, d), lambda i, e: (i, 0), **once),
            pl.BlockSpec((1, 1, d), lambda i, e: (i // tiles_per_batch, 0, 0)),
        ],
        out_specs=pl.BlockSpec((t, d), lambda i, e: (i, 0), **once),
        out_shape=jax.ShapeDtypeStruct((n, d), f32),
        scratch_shapes=[
            pltpu.VMEM((MOE_NSUB, N_EXPERTS, MOE_SUB), i32),
            pltpu.VMEM((MOE_NSUB, N_EXPERTS, MOE_SUB), f32),
            pltpu.VMEM((MOE_NSUB, N_EXPERTS, MOE_SUB), f32),
            pltpu.VMEM((MOE_NSUB, MOE_CAP, d), bf16),
            pltpu.VMEM((MOE_NSUB, MOE_CAP, MOE_SUB), bf16),
            pltpu.SMEM((MOE_NSUB,), i32),
            pltpu.SMEM((MOE_NSUB * N_EXPERTS,), i32),
        ],
        compiler_params=_cparams(2),
        name="moe",
    )(h2, per_sub(idx), per_sub(wt), wgu, bgu, wd, bd, x1, g2)


def _layer(x, c, ctx, c_ctx, w_ada, b_ada, norm1_g, norm2_g, w_in, q_norm_g, k_norm_g, rpb, conv_w, conv_b, filt_w1,
           filt_b1, filt_freq1, filt_w2, filt_b2, filt_freq2, filt_w3, hyena_skip, w_branch_a, w_branch_b, w_out,
           w_router, b_router, w_gate_up, b_gate_up, w_down, b_down):
    batch, seq, d = x.shape
    n = batch * seq
    n_ctx = ctx.shape[1]
    assert d == D_MODEL and n_ctx == CTX_LEN and seq % (A_CHUNK * LANES) == 0 and seq % MOE_TILE == 0

    rows = 8 * ((batch + 1 + 7) // 8)
    cv = jnp.zeros((rows, d), f32).at[:batch].set(c).at[batch].set(c_ctx)
    mod = _adaln(cv, w_ada, b_ada.reshape(1, -1))
    sh1, sc1, g1, sh2, sc2, g2 = [mod[:batch, i * d:(i + 1) * d].reshape(batch, 1, d) for i in range(N_MOD)]
    modc = jnp.broadcast_to(mod[batch:batch + 1], (batch, N_MOD * d))
    sh1c, sc1c = modc[:, :d].reshape(batch, 1, d), modc[:, d:2 * d].reshape(batch, 1, d)

    w_in_bf = w_in.astype(bf16)
    ones = jnp.asarray(np.kron(np.eye(2), np.ones((HEAD_DIM, HEAD_DIM))), bf16)
    qg = (jnp.tile(q_norm_g.astype(f32), N_HEADS) * (HEAD_DIM ** -0.5)).reshape(1, d)
    kg = jnp.tile(k_norm_g.astype(f32), N_HEADS).reshape(1, d)
    n1g = norm1_g.reshape(1, d)

    (ckv,) = _inproj(ctx.reshape(batch * n_ctx, d), n1g, sh1c, sc1c, w_in_bf, (kg,), ones, 2, 0, 0, 1, n_ctx, 1)
    tm = 512
    x2 = x.reshape(n, d)
    tm_in = 1024
    qkv, u, gates = _inproj(x2, n1g, sh1, sc1, w_in_bf, (qg, kg), ones, 3, 3, 2, 0, tm_in, seq // tm_in)

    oa3 = _attention(qkv, ckv, _slab_bias_tables(rpb), batch, seq)

    dc = _dft_consts(seq)
    kf = _hyena_filter(seq, filt_w1, filt_b1, filt_freq1, filt_w2, filt_b2, filt_freq2, filt_w3)
    khat = _filter_spectrum(kf, dc)
    zt, x0 = _hyena_pre(u, conv_w, conv_b, batch, seq)
    yc = _fftconv(zt, khat, hyena_skip, dc)

    x1, h2, idx, wt = _merge(oa3, yc, x0, gates, x2, g1, w_branch_a.astype(bf16), w_branch_b.astype(bf16),
                             w_out.astype(bf16), norm2_g.reshape(1, d), sh2, sc2, w_router.astype(f32).T,
                             b_router.reshape(-1, 1).astype(f32), tm, seq // tm)
    out = _moe(h2, idx, wt, w_gate_up.astype(bf16), b_gate_up.reshape(N_EXPERTS, 1, -1).astype(f32),
               w_down.astype(bf16), b_down.reshape(N_EXPERTS, 1, -1).astype(f32), x1, g2, seq // MOE_TILE)
    return out.reshape(batch, seq, d)


def kernel(x, c, ctx, c_ctx, w_ada, b_ada, norm1_g, norm2_g, w_in, q_norm_g, k_norm_g, rpb, conv_w, conv_b, filt_w1, filt_b1, filt_freq1, filt_w2, filt_b2, filt_freq2, filt_w3, hyena_skip, w_branch_a, w_branch_b, w_out, w_router, b_router, w_gate_up, b_gate_up, w_down, b_down):
    assert w_ada.shape[0] == 1, "single-layer stack"
    return _layer(x, c, ctx, c_ctx, w_ada[0], b_ada[0], norm1_g[0], norm2_g[0], w_in[0], q_norm_g[0], k_norm_g[0],
                  rpb[0], conv_w[0], conv_b[0], filt_w1[0], filt_b1[0], filt_freq1[0], filt_w2[0], filt_b2[0],
                  filt_freq2[0], filt_w3[0], hyena_skip[0], w_branch_a[0], w_branch_b[0], w_out[0], w_router[0],
                  b_router[0], w_gate_up[0], b_gate_up[0], w_down[0], b_down[0])
```

```python
import functools
import math

import numpy as np
import jax
import jax.numpy as jnp
from jax import lax
from jax.experimental import pallas as pl
from jax.experimental.pallas import tpu as pltpu

f32, bf16, i32 = jnp.float32, jnp.bfloat16, jnp.int32
HI = lax.Precision.HIGHEST

LANES = 128
D_MODEL = 1024
N_HEADS = 16
HEAD_DIM = 64
N_PAIRS = N_HEADS // 2
GRID_W = 64
WIN_ROWS = 8
WIN_COLS = 16
CTX_LEN = 256
N_MOD = 6
RMS_EPS = 1e-6
D_B = 1024
EMB_DIM = 33
FILT_HID = 64
N_BANDS = (EMB_DIM - 1) // 2
DECAY_TARGET = 1e-2
FAST_DECAY_PCT = 0.3
SLOW_DECAY_PCT = 1.5
N_EXPERTS = 32
TOP_K = 4
D_FF = 1024
SWIGLU_LIMIT = 7.0
SWIGLU_ALPHA = 1.702
NEG_BIG = -1e30

QROWS = 8
KROWS = 16
SLAB_ROWS = 10
N_SLAB_TYPES = 5
VMEM_LIMIT = 56 * 1024 * 1024


def _cparams(ndim, vmem=VMEM_LIMIT):
    return pltpu.CompilerParams(dimension_semantics=("arbitrary",) * ndim, vmem_limit_bytes=vmem)


def _sigmoid(x):
    return 1.0 / (1.0 + jnp.exp(-x))


SPLIT3 = "split3"


def _split(v):
    hi = v.astype(bf16)
    return hi, (v - hi.astype(f32)).astype(bf16)


def _dot(a, b, precision=None):
    if precision == SPLIT3:
        (ah, al), (bh, bl) = _split(a), _split(b)
        return _dot(ah, bh) + _dot(ah, bl) + _dot(al, bh)
    return jnp.dot(a, b, precision=precision, preferred_element_type=f32)


def _dot_nt(a, b, precision=None):
    return lax.dot_general(a, b, (((1,), (1,)), ((), ())), precision=precision, preferred_element_type=f32)


def _dot_tn(a, b):
    return lax.dot_general(a, b, (((0,), (0,)), ((), ())), preferred_element_type=f32)


def _bdot(a, b, precision=None):
    return lax.dot_general(a, b, (((2,), (1,)), ((0,), (0,))), precision=precision, preferred_element_type=f32)


def _adaln_kernel(cv_ref, w_ref, b_ref, o_ref):
    cv = cv_ref[...]
    o_ref[...] = _dot(cv * _sigmoid(cv), w_ref[...], HI) + b_ref[...]


def _adaln(cv, w, b):
    rows, d = cv.shape
    n = w.shape[1]
    tn = 1024
    return pl.pallas_call(
        _adaln_kernel,
        grid=(n // tn,),
        in_specs=[
            pl.BlockSpec((rows, d), lambda j: (0, 0)),
            pl.BlockSpec((d, tn), lambda j: (0, j)),
            pl.BlockSpec((1, tn), lambda j: (0, j)),
        ],
        out_specs=pl.BlockSpec((rows, tn), lambda j: (0, j)),
        out_shape=jax.ShapeDtypeStruct((rows, n), f32),
        compiler_params=_cparams(1),
        name="adaln",
    )(cv, w, b)


def _head_norm(acc, gain, ones):
    s2 = acc * acc
    hi = s2.astype(bf16)
    lo = (s2 - hi.astype(f32)).astype(bf16)
    ssum = _dot(hi, ones) + _dot(lo, ones)
    return acc * lax.rsqrt(ssum * (1.0 / HEAD_DIM) + RMS_EPS) * gain


def _inproj_kernel(n_norm, n_hm, n_u, n_g, x_ref, g_ref, sh_ref, sc_ref, w_ref, ones_ref, *rest):
    gains, rest = rest[:n_norm], list(rest[n_norm:])
    hm_ref = rest.pop(0)
    u_ref = rest.pop(0) if n_u else None
    gate_ref = rest.pop(0) if n_g else None
    h_scr = rest.pop(0)
    j = pl.program_id(1)

    @pl.when(j == 0)
    def _():
        xf = x_ref[...]
        ms = jnp.mean(xf * xf, axis=-1, keepdims=True)
        y = xf * lax.rsqrt(ms + RMS_EPS) * g_ref[...]
        h_scr[...] = (y * (1.0 + sc_ref[0]) + sh_ref[0]).astype(bf16)

    project = lambda: _dot(h_scr[...], w_ref[...])

    @pl.when(j < n_norm)
    def _():
        gain = gains[n_norm - 1][...]
        for t in range(n_norm - 2, -1, -1):
            gain = jnp.where(j == t, gains[t][...], gain)
        half = D_MODEL // 2
        for c0 in (0, half):
            acc = _dot(h_scr[...], w_ref[:, c0:c0 + half])
            for k in range(half // LANES):
                sl = slice(c0 + k * LANES, c0 + (k + 1) * LANES)
                hm_ref[sl.start // LANES] = _head_norm(acc[:, k * LANES:(k + 1) * LANES], gain[:, sl],
                                                       ones_ref[...]).astype(bf16)

    if n_hm > n_norm:

        @pl.when((j >= n_norm) & (j < n_hm))
        def _():
            acc = project()
            for hp in range(N_PAIRS):
                hm_ref[hp] = acc[:, hp * LANES:(hp + 1) * LANES].astype(bf16)

    if n_u:

        @pl.when((j >= n_hm) & (j < n_hm + n_u))
        def _():
            u_ref[...] = project().astype(bf16)

    if n_g:

        @pl.when(j >= n_hm + n_u)
        def _():
            half = D_MODEL // 2
            for c0 in (0, half):
                gate_ref[:, c0:c0 + half] = _sigmoid(_dot(h_scr[...], w_ref[:, c0:c0 + half])).astype(bf16)


def _inproj(x2, gain, shift, scale, w_bf, norm_gains, ones, n_hm, n_u, n_g, col0, tm, tiles_per_batch):
    n, d = x2.shape
    n_norm = len(norm_gains)
    out_shapes = [jax.ShapeDtypeStruct((n_hm * N_PAIRS, n, LANES), bf16)]
    out_specs = [pl.BlockSpec((N_PAIRS, tm, LANES), lambda i, j: (jnp.clip(j, 0, n_hm - 1), i, 0))]
    for j0, ncol in ((n_hm, n_u), (n_hm + n_u, n_g)):
        if ncol:
            out_shapes.append(jax.ShapeDtypeStruct((n, d * ncol), bf16))
            out_specs.append(pl.BlockSpec((tm, d), lambda i, j, j0=j0, ncol=ncol: (i, jnp.clip(j - j0, 0, ncol - 1))))
    vec = lambda: pl.BlockSpec((1, d), lambda i, j: (0, 0))
    mod = lambda: pl.BlockSpec((1, 1, d), lambda i, j: (i // tiles_per_batch, 0, 0))
    return pl.pallas_call(
        functools.partial(_inproj_kernel, n_norm, n_hm, n_u, n_g),
        grid=(n // tm, n_hm + n_u + n_g),
        in_specs=[
            pl.BlockSpec((tm, d), lambda i, j: (i, 0)),
            vec(), mod(), mod(),
            pl.BlockSpec((d, d), lambda i, j: (0, j + col0)),
            pl.BlockSpec(ones.shape, lambda i, j: (0, 0)),
        ] + [vec() for _ in norm_gains],
        out_specs=out_specs,
        out_shape=out_shapes,
        scratch_shapes=[pltpu.VMEM((tm, d), bf16)],
        compiler_params=_cparams(2),
        name="inproj",
    )(x2, gain, shift, scale, w_bf, ones, *norm_gains)


def _slab_bias_tables(rpb):
    n_drow = 2 * WIN_ROWS - 1
    c = np.arange(GRID_W)
    cs = np.clip(c - WIN_COLS // 2, 0, GRID_W - WIN_COLS)
    j = np.arange(GRID_W)
    col_ok = (j[None, :] >= cs[:, None]) & (j[None, :] < cs[:, None] + WIN_COLS)
    pad = GRID_W - WIN_COLS
    rp = jnp.pad(rpb.astype(f32), ((0, 0), (0, 0), (pad, pad)))
    bm = jnp.stack([rp[:, :, pad + WIN_COLS - 1 - ci:pad + WIN_COLS - 1 - ci + GRID_W] for ci in range(GRID_W)], axis=2)
    bm = jnp.where(col_ok[None, None], bm, NEG_BIG)
    return pl.pallas_call(
        functools.partial(_slab_bias_kernel, n_drow),
        grid=(N_HEADS,),
        in_specs=[pl.BlockSpec((1, n_drow, GRID_W, GRID_W), lambda h: (h, 0, 0, 0))],
        out_specs=pl.BlockSpec((1, N_SLAB_TYPES, 2 * GRID_W, SLAB_ROWS * GRID_W), lambda h: (h, 0, 0, 0)),
        out_shape=jax.ShapeDtypeStruct((N_HEADS, N_SLAB_TYPES, 2 * GRID_W, SLAB_ROWS * GRID_W), f32),
        compiler_params=_cparams(1),
        name="slabbias",
    )(bm)


_SLAB_SPEC = {0: [(3, 0, 8), (2, 1, 9)], 1: [(7, 0, 8), (6, 0, 8)], 2: [(5, 0, 8), (4, 0, 8)],
              3: [(1, 2, 10), (0, 2, 10)], 4: [(-1, 2, 10), (-2, 2, 10)]}


def _slab_bias_kernel(n_drow, bm_ref, o_ref):
    masked = jnp.full((GRID_W, GRID_W), NEG_BIG, f32)
    for t in range(N_SLAB_TYPES):
        for qi, (start, lo, hi) in enumerate(_SLAB_SPEC[t]):
            for kl in range(SLAB_ROWS):
                drow = start + kl
                ok = lo <= kl < hi and 0 <= drow < n_drow
                tile = bm_ref[0, drow] if ok else masked
                o_ref[0, t, qi * GRID_W:(qi + 1) * GRID_W, kl * GRID_W:(kl + 1) * GRID_W] = tile


ATTN_PAIRS = 4


def _attn_kernel(n_rows, q_ref, *refs):
    k_refs, v_refs = refs[:ATTN_PAIRS], refs[ATTN_PAIRS:2 * ATTN_PAIRS]
    kc_ref, vc_ref, sb_ref, o_ref = refs[2 * ATTN_PAIRS:]
    for g in range(ATTN_PAIRS):
        _attn_head_pair(n_rows, g, q_ref, k_refs[g], v_refs[g], kc_ref, vc_ref, sb_ref, o_ref)


def _attn_head_pair(n_rows, g, q_ref, k_ref, v_ref, kc_ref, vc_ref, sb_ref, o_ref):
    rb = pl.program_id(2)
    base = jnp.clip(QROWS * rb - WIN_ROWS // 2, 0, n_rows - KROWS)
    tq = q_ref.shape[1]
    pr = 2 * GRID_W
    slab = SLAB_ROWS * GRID_W
    lane = lax.broadcasted_iota(i32, (tq, LANES), 1)
    q = q_ref[g]
    zero = jnp.zeros_like(q)
    q2 = jnp.concatenate([jnp.where(lane < HEAD_DIM, q, zero), jnp.where(lane >= HEAD_DIM, q, zero)], axis=0)
    sc_all = _dot_nt(q2, kc_ref[g])
    pws, pcs, dens = [], [], []
    for ip in range(QROWS // 2):
        r0 = QROWS * rb + 2 * ip
        rs0 = jnp.clip(r0 - WIN_ROWS // 2, 0, n_rows - WIN_ROWS)
        s = jnp.minimum(rs0 - base, KROWS - SLAB_ROWS)
        typ = jnp.where(r0 < 2, 1, jnp.where(r0 < 4, 2, jnp.where(r0 == n_rows - 4, 3, jnp.where(r0 == n_rows - 2, 4, 0))))
        off = pl.multiple_of(s * GRID_W, GRID_W)
        kw = k_ref[0, pl.ds(off, slab), :]
        ra, rbb = slice(ip * pr, (ip + 1) * pr), slice(tq + ip * pr, tq + (ip + 1) * pr)
        qq = jnp.concatenate([q2[ra], q2[rbb]], axis=0)
        sw = _dot_nt(qq, kw) + jnp.concatenate([sb_ref[2 * g, typ], sb_ref[2 * g + 1, typ]], axis=0)
        sc = jnp.concatenate([sc_all[ra], sc_all[rbb]], axis=0)
        m = jnp.maximum(jnp.max(sw, axis=-1, keepdims=True), jnp.max(sc, axis=-1, keepdims=True))
        pw = jnp.exp(sw - m)
        pc = jnp.exp(sc - m)
        dens.append(jnp.sum(pw, axis=-1, keepdims=True) + jnp.sum(pc, axis=-1, keepdims=True))
        pws.append((pw.astype(bf16), off))
        pcs.append(pc.astype(bf16))
    oc = _dot(jnp.concatenate(pcs, axis=0), vc_ref[g])
    lane_p = lax.broadcasted_iota(i32, (pr, LANES), 1)
    for ip in range(QROWS // 2):
        pw, off = pws[ip]
        o = (_dot(pw, v_ref[0, pl.ds(off, slab), :]) + oc[ip * 2 * pr:(ip + 1) * 2 * pr]) / dens[ip]
        o_ref[g, ip * pr:(ip + 1) * pr, :] = jnp.where(lane_p < HEAD_DIM, o[:pr], o[pr:]).astype(bf16)


def _attention(qkv, ckv, sb, batch, seq):
    n_rows = seq // GRID_W
    assert n_rows % QROWS == 0 and n_rows >= KROWS
    nrb = n_rows // QROWS
    tq = QROWS * GRID_W
    tk = KROWS * GRID_W
    n_tok = batch * seq

    def kv_map(which, g):
        def index(b, hg, rb):
            base = jnp.clip(QROWS * rb - WIN_ROWS // 2, 0, n_rows - KROWS)
            return (which * N_PAIRS + hg * ATTN_PAIRS + g, (b * n_rows + base) * GRID_W, 0)
        return index

    kv_specs = lambda which: [pl.BlockSpec((pl.Element(1), pl.Element(tk), pl.Element(LANES)), kv_map(which, g))
                              for g in range(ATTN_PAIRS)]
    n_groups = N_PAIRS // ATTN_PAIRS
    ctx_spec = lambda which: pl.BlockSpec((ATTN_PAIRS, CTX_LEN, LANES),
                                          lambda b, hg, rb: (which * n_groups + hg, b, 0))
    return pl.pallas_call(
        functools.partial(_attn_kernel, n_rows),
        grid=(batch, n_groups, nrb),
        in_specs=[pl.BlockSpec((ATTN_PAIRS, tq, LANES), lambda b, hg, rb: (hg, b * nrb + rb, 0))]
        + kv_specs(1) + kv_specs(2) + [
            ctx_spec(0), ctx_spec(1),
            pl.BlockSpec((2 * ATTN_PAIRS, N_SLAB_TYPES, 2 * GRID_W, SLAB_ROWS * GRID_W), lambda b, hg, rb: (hg, 0, 0, 0),
                         pipeline_mode=pl.Buffered(1)),
        ],
        out_specs=pl.BlockSpec((ATTN_PAIRS, tq, LANES), lambda b, hg, rb: (hg, b * nrb + rb, 0)),
        out_shape=jax.ShapeDtypeStruct((N_PAIRS, n_tok, LANES), bf16),
        compiler_params=_cparams(3),
        name="attn",
    )(qkv, *([qkv] * (2 * ATTN_PAIRS)), ckv, ckv, sb)


A_CHUNK = 8


def _filter_kernel(seq, w1t_ref, w1c_ref, w1s_ref, b1_ref, fr1_ref, w2t_ref, b2_ref, fr2_ref, w3t_ref, dl_ref,
                   fq_ref, o_ref):
    st = pl.program_id(0)
    npos = A_CHUNK * LANES
    n = st * npos + lax.broadcasted_iota(i32, (1, npos), 1)
    idx = jnp.where(n < seq, n, 2 * seq - n).astype(f32)
    t = idx / (seq - 1)
    ang = (2.0 * math.pi / seq) * idx
    fa = fq_ref[...] * ang
    z1 = w1t_ref[...] * t + _dot(w1c_ref[...], jnp.cos(fa), HI) + _dot(w1s_ref[...], -jnp.sin(fa), HI) + b1_ref[...]
    h1 = jnp.sin(fr1_ref[...] * z1)
    h2 = jnp.sin(fr2_ref[...] * (_dot(w2t_ref[...], h1, HI) + b2_ref[...]))
    filt = _dot(w3t_ref[0], h2, SPLIT3) * jnp.exp(-t * dl_ref[...])
    filt = jnp.where(n == seq, 0.0, filt)
    for al in range(A_CHUNK):
        o_ref[0, pl.ds(al, D_B, stride=A_CHUNK), :] = filt[:, al * LANES:(al + 1) * LANES]


def _from_chunks(blk, db):
    cat = lambda d: jnp.concatenate([blk[c, d * A_CHUNK:(d + 1) * A_CHUNK] for c in range(blk.shape[0])], axis=0)
    return jnp.stack([cat(d) for d in range(db)])


def _to_chunks(y):
    cat = lambda c: jnp.concatenate([y[d, c * A_CHUNK:(c + 1) * A_CHUNK] for d in range(y.shape[0])], axis=0)
    return jnp.stack([cat(c) for c in range(y.shape[1] // A_CHUNK)])


def _hyena_filter(seq, w1, b1, fr1, w2, b2, fr2, w3):
    na = 2 * seq // LANES
    steps = na // A_CHUNK
    fwd_steps = steps // 2
    col = lambda v: v.reshape(-1, 1).astype(f32)
    deltas = jnp.abs(jnp.linspace(math.log(DECAY_TARGET) / SLOW_DECAY_PCT, math.log(DECAY_TARGET) / FAST_DECAY_PCT,
                                  D_B, dtype=f32))
    freqs = jnp.linspace(1e-4, N_BANDS - 1, N_BANDS, dtype=f32)
    w3t = w3.astype(f32).T.reshape(2, D_B, FILT_HID)
    w1 = w1.astype(f32)
    args = (w1[0:1].T, w1[1:1 + N_BANDS].T, w1[1 + N_BANDS:].T, col(b1), col(fr1), w2.astype(f32).T, col(b2), col(fr2),
            w3t, col(deltas), col(freqs))
    full = lambda a: pl.BlockSpec(a.shape, lambda s: (0,) * a.ndim)
    in_specs = [full(a) for a in args]
    in_specs[8] = pl.BlockSpec((1, D_B, FILT_HID), lambda s: (s // fwd_steps, 0, 0))
    return pl.pallas_call(
        functools.partial(_filter_kernel, seq),
        grid=(steps,),
        in_specs=in_specs,
        out_specs=pl.BlockSpec((1, D_B * A_CHUNK, LANES), lambda s: (s, 0, 0)),
        out_shape=jax.ShapeDtypeStruct((steps, D_B * A_CHUNK, LANES), f32),
        compiler_params=_cparams(1),
        name="filt",
    )(*args)


def _dft_consts(seq):
    n = 2 * seq
    n1 = n // LANES
    a_used = seq // LANES
    k1 = np.arange(n1)[:, None]
    a = np.arange(n1)[None, :]
    ang1 = -2.0 * np.pi * ((k1 * a) % n1) / n1
    f1 = np.concatenate([np.cos(ang1), np.sin(ang1)], axis=0)
    b = np.arange(LANES)[None, :]
    angt = -2.0 * np.pi * ((k1 * b) % n) / n
    tw_r, tw_i = np.cos(angt), np.sin(angt)
    bb = np.arange(LANES)[:, None]
    k2 = np.arange(LANES)[None, :]
    angg = -2.0 * np.pi * ((bb * k2) % LANES) / LANES
    gr, gi = np.cos(angg), np.sin(angg)
    g_fwd = np.block([[gr, gi], [-gi, gr]])
    g_inv = np.block([[gr, -gi], [gi, gr]])
    f1_inv = np.concatenate([np.cos(ang1).T[:a_used], np.sin(ang1).T[:a_used]], axis=1)
    c = lambda x: jnp.asarray(x, f32)
    return dict(n1=n1, a_used=a_used, f1_full=c(f1), f1=c(f1[:, :a_used]), tw_r=c(tw_r), tw_i=c(tw_i), g_fwd=c(g_fwd),
                g_inv=c(g_inv), f1_inv=c(f1_inv))


def _pair_dot(m, x, precision=None):
    out = []
    for p in range(x.shape[0] // 2):
        y2 = _dot(m, jnp.concatenate([x[2 * p], x[2 * p + 1]], axis=1), precision)
        out += [y2[:, :LANES], y2[:, LANES:]]
    return jnp.stack(out)


def _fft_fwd(x, f1, tw_r, tw_i, g_fwd, precision, cast):
    db = x.shape[0]
    n1 = tw_r.shape[0]
    y = _pair_dot(cast(f1), cast(x), precision)
    yr, yi = y[:, :n1], y[:, n1:]
    ytr = yr * tw_r - yi * tw_i
    yti = yr * tw_i + yi * tw_r
    lhs = jnp.concatenate([ytr, yti], axis=-1).reshape(db * n1, 2 * LANES)
    return _dot(cast(lhs), cast(g_fwd), precision)


def _fftk_kernel(x_ref, f1_ref, twr_ref, twi_ref, g_ref, o_ref):
    db, n1 = o_ref.shape[0], twr_ref.shape[0]
    x = _from_chunks(x_ref[...], db)
    z = _fft_fwd(x, f1_ref[...], twr_ref[...], twi_ref[...], g_ref[...], SPLIT3, lambda v: v)
    o_ref[...] = z.reshape(db, n1, 2 * LANES)


def _filter_spectrum(kf, dc):
    n_chunks = kf.shape[0]
    d, n1 = kf.shape[1] // A_CHUNK, n_chunks * A_CHUNK
    db = 16
    full = lambda a: pl.BlockSpec(a.shape, lambda i: (0,) * a.ndim)
    consts = (dc["f1_full"], dc["tw_r"], dc["tw_i"], dc["g_fwd"])
    return pl.pallas_call(
        _fftk_kernel,
        grid=(d // db,),
        in_specs=[pl.BlockSpec((n_chunks, db * A_CHUNK, LANES), lambda i: (0, i, 0))] + [full(a) for a in consts],
        out_specs=pl.BlockSpec((db, n1, 2 * LANES), lambda i: (i, 0, 0)),
        out_shape=jax.ShapeDtypeStruct((d, n1, 2 * LANES), f32),
        compiler_params=_cparams(1),
        name="fftk",
    )(kf, *consts)


def _hpre_kernel(n_chunks, u_ref, up_ref, un_ref, cw_ref, cb_ref, zt_ref, x0_ref):
    ac = pl.program_id(1)
    tm = u_ref.shape[0]
    has_prev = (ac > 0).astype(f32)
    has_next = (ac < n_chunks - 1).astype(f32)
    row = lax.broadcasted_iota(i32, (tm, LANES), 0)
    halo = up_ref.shape[0]

    def conv(c0):
        sl = slice(c0, c0 + LANES)
        u = u_ref[:, sl].astype(f32)
        prev = up_ref[halo - 1:halo, sl].astype(f32) * has_prev
        nxt = un_ref[0:1, sl].astype(f32) * has_next
        um = jnp.where(row == 0, prev, pltpu.roll(u, 1, 0))
        upl = jnp.where(row == tm - 1, nxt, pltpu.roll(u, tm - 1, 0))
        return um * cw_ref[0:1, sl] + u * cw_ref[1:2, sl] + upl * cw_ref[2:3, sl] + cb_ref[:, sl]

    for dt in range(D_B // LANES):
        v = conv(dt * LANES)
        x1 = conv(D_B + dt * LANES)
        x0_ref[:, dt * LANES:(dt + 1) * LANES] = conv(2 * D_B + dt * LANES).astype(bf16)
        z = v * x1
        for al in range(A_CHUNK):
            rows = pl.ds(dt * LANES * A_CHUNK + al, LANES, stride=A_CHUNK)
            zt_ref[0, 0, rows, :] = z[al * LANES:(al + 1) * LANES, :].T


def _hyena_pre(u, conv_w, conv_b, batch, seq):
    n = u.shape[0]
    tm = A_CHUNK * LANES
    n_chunks = seq // tm
    halo = 16
    hb = tm // halo
    cw = jnp.concatenate([conv_w.astype(f32), jnp.zeros((8 - conv_w.shape[0], conv_w.shape[1]), f32)], axis=0)
    return pl.pallas_call(
        functools.partial(_hpre_kernel, n_chunks),
        grid=(batch, n_chunks),
        in_specs=[
            pl.BlockSpec((tm, 3 * D_B), lambda b, c: (b * n_chunks + c, 0)),
            pl.BlockSpec((halo, 3 * D_B), lambda b, c: (jnp.maximum((b * n_chunks + c) * hb - 1, 0), 0)),
            pl.BlockSpec((halo, 3 * D_B), lambda b, c: (jnp.minimum((b * n_chunks + c + 1) * hb, n // halo - 1), 0)),
            pl.BlockSpec((8, 3 * D_B), lambda b, c: (0, 0)),
            pl.BlockSpec((1, 3 * D_B), lambda b, c: (0, 0)),
        ],
        out_specs=[
            pl.BlockSpec((1, 1, D_B * A_CHUNK, LANES), lambda b, c: (b, c, 0, 0)),
            pl.BlockSpec((tm, D_B), lambda b, c: (b * n_chunks + c, 0)),
        ],
        out_shape=[
            jax.ShapeDtypeStruct((batch, n_chunks, D_B * A_CHUNK, LANES), f32),
            jax.ShapeDtypeStruct((n, D_B), bf16),
        ],
        compiler_params=_cparams(2),
        name="hpre",
    )(u, u, u, cw, conv_b.reshape(1, -1).astype(f32))


def _fftconv_kernel(z_ref, kh_ref, skip_ref, f1_ref, twr_ref, twi_ref, gf_ref, gi_ref, f1i_ref, y_ref):
    db = skip_ref.shape[0]
    n1 = twr_ref.shape[0]
    tw_r, tw_i = twr_ref[...], twi_ref[...]
    cast = lambda v: v.astype(bf16)
    z = _from_chunks(z_ref[0], db)
    spec = _fft_fwd(z, f1_ref[...], tw_r, tw_i, gf_ref[...], None, cast)
    kh = kh_ref[...].reshape(db * n1, 2 * LANES)
    sr, si = spec[:, :LANES], spec[:, LANES:]
    kr, ki = kh[:, :LANES], kh[:, LANES:]
    prod = jnp.concatenate([sr * kr - si * ki, sr * ki + si * kr], axis=-1)
    q = _dot(cast(prod), cast(gi_ref[...])).reshape(db, n1, 2 * LANES)
    qr, qi = q[:, :, :LANES], q[:, :, LANES:]
    rhs = jnp.concatenate([qr * tw_r + qi * tw_i, qi * tw_r - qr * tw_i], axis=1)
    y = _pair_dot(cast(f1i_ref[...]), cast(rhs)) * (1.0 / (n1 * LANES))
    y_ref[0] = _to_chunks(y + z * skip_ref[...])


def _fftconv(zt, khat, skip, dc):
    batch, n_chunks, rows, _ = zt.shape
    d = rows // A_CHUNK
    n1 = dc["n1"]
    db = 64
    consts = (dc["f1"], dc["tw_r"], dc["tw_i"], dc["g_fwd"], dc["g_inv"], dc["f1_inv"])
    full = lambda a: pl.BlockSpec(a.shape, lambda i, b: (0,) * a.ndim)
    blk = lambda: pl.BlockSpec((1, n_chunks, db * A_CHUNK, LANES), lambda i, b: (b, 0, i, 0))
    return pl.pallas_call(
        _fftconv_kernel,
        grid=(d // db, batch),
        in_specs=[
            blk(),
            pl.BlockSpec((db, n1, 2 * LANES), lambda i, b: (i, 0, 0)),
            pl.BlockSpec((db, 1, 1), lambda i, b: (i, 0, 0)),
        ] + [full(a) for a in consts],
        out_specs=blk(),
        out_shape=jax.ShapeDtypeStruct(zt.shape, f32),
        compiler_params=_cparams(2),
        name="fftconv",
    )(zt, khat, skip.reshape(d, 1, 1).astype(f32), *consts)


IDX_ROWS = 8


def _merge_kernel(oa_ref, yc_ref, x0_ref, ga_ref, gb_ref, x_ref, g1_ref, wba_ref, wbb_ref, wout_ref, n2g_ref, sh2_ref,
                  sc2_ref, wrt_ref, br_ref, x1_ref, h2_ref, idx_ref, wt_ref, ob_scr):
    a_tile = x0_ref.shape[0] // LANES
    part = pl.program_id(0) % (A_CHUNK // a_tile)
    for dt in range(D_B // LANES):
        for al in range(a_tile):
            rows = slice(al * LANES, (al + 1) * LANES)
            cols = slice(dt * LANES, (dt + 1) * LANES)
            src = pl.ds(dt * LANES * A_CHUNK + part * a_tile + al, LANES, stride=A_CHUNK)
            ob_scr[rows, cols] = (x0_ref[rows, cols].astype(f32) * yc_ref[0, 0, src, :].T).astype(bf16)
    oa = jnp.concatenate([oa_ref[hp] for hp in range(N_PAIRS)], axis=1)
    m = ga_ref[...].astype(f32) * _dot(oa, wba_ref[...]) + gb_ref[...].astype(f32) * _dot(ob_scr[...], wbb_ref[...])
    x1 = x_ref[...] + g1_ref[0] * _dot(m.astype(bf16), wout_ref[...])
    x1_ref[...] = x1
    ms = jnp.mean(x1 * x1, axis=-1, keepdims=True)
    h2 = (x1 * lax.rsqrt(ms + RMS_EPS) * n2g_ref[...]) * (1.0 + sc2_ref[0]) + sh2_ref[0]
    h_hi = h2.astype(bf16)
    h2_ref[...] = h_hi
    h_lo = (h2 - h_hi.astype(f32)).astype(bf16)
    wr = wrt_ref[...]
    w_hi = wr.astype(bf16)
    w_lo = (wr - w_hi.astype(f32)).astype(bf16)
    logits = _dot_nt(w_hi, h_hi) + _dot_nt(w_hi, h_lo) + _dot_nt(w_lo, h_hi) + br_ref[...]
    eio = lax.broadcasted_iota(i32, logits.shape, 0)
    vals = logits
    idxs, tops = [], []
    for _ in range(TOP_K):
        mx = jnp.max(vals, axis=0, keepdims=True)
        ix = jnp.min(jnp.where(vals == mx, eio, N_EXPERTS), axis=0, keepdims=True)
        idxs.append(ix)
        tops.append(mx)
        vals = jnp.where(eio == ix, -jnp.inf, vals)
    ex = [jnp.exp(v - tops[0]) for v in tops]
    den = ex[0] + ex[1] + ex[2] + ex[3]
    idx_ref[...] = jnp.concatenate(idxs + [jnp.full_like(idxs[0], -1)] * (IDX_ROWS - TOP_K), axis=0)
    wt_ref[...] = jnp.concatenate([e / den for e in ex] + [jnp.zeros_like(den)] * (IDX_ROWS - TOP_K), axis=0)


def _merge(oa3, yc, x0, gates, x2, g1, wba, wbb, wout, n2g, sh2, sc2, wrt, br, tm, tiles_per_batch):
    n, d = x2.shape
    parts = A_CHUNK * LANES // tm
    tok = lambda col=0: pl.BlockSpec((tm, d), lambda i: (i, col))
    mod = lambda: pl.BlockSpec((1, 1, d), lambda i: (i // tiles_per_batch, 0, 0))
    wsp = lambda: pl.BlockSpec((d, d), lambda i: (0, 0))
    ga, gb = gates, gates
    return pl.pallas_call(
        _merge_kernel,
        grid=(n // tm,),
        in_specs=[
            pl.BlockSpec((N_PAIRS, tm, LANES), lambda i: (0, i, 0)),
            pl.BlockSpec((1, 1, D_B * A_CHUNK, LANES),
                         lambda i: (i // tiles_per_batch, (i % tiles_per_batch) // parts, 0, 0)),
            tok(), tok(0), tok(1), tok(), mod(), wsp(), wsp(), wsp(),
            pl.BlockSpec((1, d), lambda i: (0, 0)), mod(), mod(),
            pl.BlockSpec((N_EXPERTS, d), lambda i: (0, 0)),
            pl.BlockSpec((N_EXPERTS, 1), lambda i: (0, 0)),
        ],
        out_specs=[tok(), tok(), pl.BlockSpec((IDX_ROWS, tm), lambda i: (0, i)), pl.BlockSpec((IDX_ROWS, tm), lambda i: (0, i))],
        out_shape=[
            jax.ShapeDtypeStruct((n, d), f32),
            jax.ShapeDtypeStruct((n, d), bf16),
            jax.ShapeDtypeStruct((IDX_ROWS, n), i32),
            jax.ShapeDtypeStruct((IDX_ROWS, n), f32),
        ],
        scratch_shapes=[pltpu.VMEM((tm, d), bf16)],
        compiler_params=_cparams(1),
        name="merge",
    )(oa3, yc, x0, ga, gb, x2, g1, wba, wbb, wout, n2g, sh2, sc2, wrt, br)


MOE_SUB = 1024
MOE_NSUB = 2
MOE_TILE = MOE_SUB * MOE_NSUB
MOE_ROWS = 160
MOE_CAP = 12 * MOE_ROWS


def _moe_kernel(h2_ref, idx_ref, wt_ref, wgu_ref, bgu_ref, wd_ref, bd_ref, x1_ref, g2_ref, y_ref, rank_scr, msk_scr,
                wte_scr, o_scr, p_scr, st_ref, cnt_ref):
    e = pl.program_id(1)
    ts = MOE_SUB
    last = e == N_EXPERTS - 1

    @pl.when(e == 0)
    def _():
        y_ref[...] = jnp.zeros_like(y_ref)
        o_scr[...] = jnp.zeros_like(o_scr)
        p_scr[...] = jnp.zeros_like(p_scr)
        eio = lax.broadcasted_iota(i32, (N_EXPERTS, ts), 0)
        lane = lax.broadcasted_iota(i32, (N_EXPERTS, ts), 1)

        for s in range(MOE_NSUB):
            idx = idx_ref[s]
            w = wt_ref[s]
            msk = jnp.zeros((N_EXPERTS, ts), f32)
            wte = jnp.zeros((N_EXPERTS, ts), f32)
            for k in range(TOP_K):
                hit = idx[k:k + 1, :] == eio
                msk = msk + hit.astype(f32)
                wte = wte + jnp.where(hit, w[k:k + 1, :], 0.0)
            csum = msk
            sh = 1
            while sh < ts:
                csum = csum + jnp.where(lane >= sh, pltpu.roll(csum, sh, 1), 0.0)
                sh *= 2
            rank_scr[s] = (csum - msk).astype(i32)
            msk_scr[s] = msk
            wte_scr[s] = wte
            st_ref[s] = 0
            for ex in range(N_EXPERTS):
                cnt_ref[s * N_EXPERTS + ex] = jnp.sum(csum[ex:ex + 1, ts - 1:ts]).astype(i32)

    subs = range(MOE_NSUB)
    toks = [slice(s * ts, (s + 1) * ts) for s in subs]
    ranks = [rank_scr[s, pl.ds(e, 1), :] for s in subs]
    sels = [msk_scr[s, pl.ds(e, 1), :] > 0.0 for s in subs]
    wtes = [wte_scr[s, pl.ds(e, 1), :] for s in subs]
    n_chunks = [(cnt_ref[s * N_EXPERTS + e] + MOE_ROWS - 1) // MOE_ROWS for s in subs]
    max_chunks = functools.reduce(jnp.maximum, n_chunks)

    def combine(s):
        y_ref[toks[s], :] += _dot_tn(p_scr[s], o_scr[s])
        o_scr[s] = jnp.zeros((MOE_CAP, D_MODEL), bf16)
        st_ref[s] = 0

    def trip(jc, carry):
        live = [jc < n for n in n_chunks]
        for s in subs:
            pl.when((live[s] & (st_ref[s] + MOE_ROWS > MOE_CAP)) | (jc >= max_chunks))(functools.partial(combine, s))

        @pl.when(jc < max_chunks)
        def _():
            jio = lax.broadcasted_iota(i32, (MOE_ROWS, ts), 0)
            hits = [((ranks[s] - jc * MOE_ROWS) == jio) & sels[s] for s in subs]
            onehots = [h.astype(bf16) for h in hits]
            xg = jnp.concatenate([_dot(onehots[s], h2_ref[toks[s], :]).astype(bf16) for s in subs], axis=0)
            gu = _dot(xg, wgu_ref[0]) + bgu_ref[0]
            gate = jnp.minimum(gu[:, :D_FF], SWIGLU_LIMIT)
            up = jnp.clip(gu[:, D_FF:], -SWIGLU_LIMIT, SWIGLU_LIMIT)
            act = (up + 1.0) * gate * _sigmoid(SWIGLU_ALPHA * gate)
            o = _dot(act.astype(bf16), wd_ref[0]) + bd_ref[0]
            for s in subs:

                @pl.when(live[s])
                def _(s=s):
                    fill = pl.multiple_of(st_ref[s], 16)
                    wrow = jnp.sum(jnp.where(hits[s], wtes[s], 0.0), axis=1, keepdims=True)
                    o_scr[s, pl.ds(fill, MOE_ROWS), :] = (o[s * MOE_ROWS:(s + 1) * MOE_ROWS] * wrow).astype(bf16)
                    p_scr[s, pl.ds(fill, MOE_ROWS), :] = onehots[s]
                    st_ref[s] = fill + MOE_ROWS

        return carry

    lax.fori_loop(0, max_chunks + last.astype(i32), trip, 0)

    @pl.when(last)
    def _():
        y_ref[...] = x1_ref[...] + g2_ref[0] * y_ref[...]


def _moe(h2, idx, wt, wgu, bgu, wd, bd, x1, g2, tiles_per_batch):
    n, d = h2.shape
    t = MOE_TILE
    once = dict(pipeline_mode=pl.Buffered(1))
    per_sub = lambda a: jnp.transpose(a.reshape(IDX_ROWS, n // MOE_SUB, MOE_SUB), (1, 0, 2))
    route_spec = lambda: pl.BlockSpec((MOE_NSUB, IDX_ROWS, MOE_SUB), lambda i, e: (i, 0, 0), **once)
    return pl.pallas_call(
        _moe_kernel,
        grid=(n // t, N_EXPERTS),
        in_specs=[
            pl.BlockSpec((t, d), lambda i, e: (i, 0), **once),
            route_spec(), route_spec(),
            pl.BlockSpec((1, d, 2 * D_FF), lambda i, e: (e, 0, 0)),
            pl.BlockSpec((1, 1, 2 * D_FF), lambda i, e: (e, 0, 0)),
            pl.BlockSpec((1, D_FF, d), lambda i, e: (e, 0, 0)),
            pl.BlockSpec((1, 1, d), lambda i, e: (e, 0, 0)),
            pl.BlockSpec((t, d), lambda i, e: (i, 0), **once),
            pl.BlockSpec((1, 1, d), lambda i, e: (i // tiles_per_batch, 0, 0)),
        ],
        out_specs=pl.BlockSpec((t, d), lambda i, e: (i, 0), **once),
        out_shape=jax.ShapeDtypeStruct((n, d), f32),
        scratch_shapes=[
            pltpu.VMEM((MOE_NSUB, N_EXPERTS, MOE_SUB), i32),
            pltpu.VMEM((MOE_NSUB, N_EXPERTS, MOE_SUB), f32),
            pltpu.VMEM((MOE_NSUB, N_EXPERTS, MOE_SUB), f32),
            pltpu.VMEM((MOE_NSUB, MOE_CAP, d), bf16),
            pltpu.VMEM((MOE_NSUB, MOE_CAP, MOE_SUB), bf16),
            pltpu.SMEM((MOE_NSUB,), i32),
            pltpu.SMEM((MOE_NSUB * N_EXPERTS,), i32),
        ],
        compiler_params=_cparams(2),
        name="moe",
    )(h2, per_sub(idx), per_sub(wt), wgu, bgu, wd, bd, x1, g2)


def _layer(x, c, ctx, c_ctx, w_ada, b_ada, norm1_g, norm2_g, w_in, q_norm_g, k_norm_g, rpb, conv_w, conv_b, filt_w1,
           filt_b1, filt_freq1, filt_w2, filt_b2, filt_freq2, filt_w3, hyena_skip, w_branch_a, w_branch_b, w_out,
           w_router, b_router, w_gate_up, b_gate_up, w_down, b_down):
    batch, seq, d = x.shape
    n = batch * seq
    n_ctx = ctx.shape[1]
    assert d == D_MODEL and n_ctx == CTX_LEN and seq % (A_CHUNK * LANES) == 0 and seq % MOE_TILE == 0

    rows = 8 * ((batch + 1 + 7) // 8)
    cv = jnp.zeros((rows, d), f32).at[:batch].set(c).at[batch].set(c_ctx)
    mod = _adaln(cv, w_ada, b_ada.reshape(1, -1))
    sh1, sc1, g1, sh2, sc2, g2 = [mod[:batch, i * d:(i + 1) * d].reshape(batch, 1, d) for i in range(N_MOD)]
    modc = jnp.broadcast_to(mod[batch:batch + 1], (batch, N_MOD * d))
    sh1c, sc1c = modc[:, :d].reshape(batch, 1, d), modc[:, d:2 * d].reshape(batch, 1, d)

    w_in_bf = w_in.astype(bf16)
    ones = jnp.asarray(np.kron(np.eye(2), np.ones((HEAD_DIM, HEAD_DIM))), bf16)
    qg = (jnp.tile(q_norm_g.astype(f32), N_HEADS) * (HEAD_DIM ** -0.5)).reshape(1, d)
    kg = jnp.tile(k_norm_g.astype(f32), N_HEADS).reshape(1, d)
    n1g = norm1_g.reshape(1, d)

    (ckv,) = _inproj(ctx.reshape(batch * n_ctx, d), n1g, sh1c, sc1c, w_in_bf, (kg,), ones, 2, 0, 0, 1, n_ctx, 1)
    tm = 512
    x2 = x.reshape(n, d)
    tm_in = 1024
    qkv, u, gates = _inproj(x2, n1g, sh1, sc1, w_in_bf, (qg, kg), ones, 3, 3, 2, 0, tm_in, seq // tm_in)

    oa3 = _attention(qkv, ckv, _slab_bias_tables(rpb), batch, seq)

    dc = _dft_consts(seq)
    kf = _hyena_filter(seq, filt_w1, filt_b1, filt_freq1, filt_w2, filt_b2, filt_freq2, filt_w3)
    khat = _filter_spectrum(kf, dc)
    zt, x0 = _hyena_pre(u, conv_w, conv_b, batch, seq)
    yc = _fftconv(zt, khat, hyena_skip, dc)

    x1, h2, idx, wt = _merge(oa3, yc, x0, gates, x2, g1, w_branch_a.astype(bf16), w_branch_b.astype(bf16),
                             w_out.astype(bf16), norm2_g.reshape(1, d), sh2, sc2, w_router.astype(f32).T,
                             b_router.reshape(-1, 1).astype(f32), tm, seq // tm)
    out = _moe(h2, idx, wt, w_gate_up.astype(bf16), b_gate_up.reshape(N_EXPERTS, 1, -1).astype(f32),
               w_down.astype(bf16), b_down.reshape(N_EXPERTS, 1, -1).astype(f32), x1, g2, seq // MOE_TILE)
    return out.reshape(batch, seq, d)


def kernel(x, c, ctx, c_ctx, w_ada, b_ada, norm1_g, norm2_g, w_in, q_norm_g, k_norm_g, rpb, conv_w, conv_b, filt_w1, filt_b1, filt_freq1, filt_w2, filt_b2, filt_freq2, filt_w3, hyena_skip, w_branch_a, w_branch_b, w_out, w_router, b_router, w_gate_up, b_gate_up, w_down, b_down):
    assert w_ada.shape[0] == 1, "single-layer stack"
    return _layer(x, c, ctx, c_ctx, w_ada[0], b_ada[0], norm1_g[0], norm2_g[0], w_in[0], q_norm_g[0], k_norm_g[0],
                  rpb[0], conv_w[0], conv_b[0], filt_w1[0], filt_b1[0], filt_freq1[0], filt_w2[0], filt_b2[0],
                  filt_freq2[0], filt_w3[0], hyena_skip[0], w_branch_a[0], w_branch_b[0], w_out[0], w_router[0],
                  b_router[0], w_gate_up[0], b_gate_up[0], w_down[0], b_down[0])
```

```python
import functools
import math

import numpy as np
import jax
import jax.numpy as jnp
from jax import lax
from jax.experimental import pallas as pl
from jax.experimental.pallas import tpu as pltpu

f32, bf16, i32 = jnp.float32, jnp.bfloat16, jnp.int32
HI = lax.Precision.HIGHEST

LANES = 128
D_MODEL = 1024
N_HEADS = 16
HEAD_DIM = 64
N_PAIRS = N_HEADS // 2
GRID_W = 64
WIN_ROWS = 8
WIN_COLS = 16
CTX_LEN = 256
N_MOD = 6
RMS_EPS = 1e-6
D_B = 1024
EMB_DIM = 33
FILT_HID = 64
N_BANDS = (EMB_DIM - 1) // 2
DECAY_TARGET = 1e-2
FAST_DECAY_PCT = 0.3
SLOW_DECAY_PCT = 1.5
N_EXPERTS = 32
TOP_K = 4
D_FF = 1024
SWIGLU_LIMIT = 7.0
SWIGLU_ALPHA = 1.702
NEG_BIG = -1e30

QROWS = 8
KROWS = 16
SLAB_ROWS = 10
N_SLAB_TYPES = 5
VMEM_LIMIT = 56 * 1024 * 1024


def _cparams(ndim, vmem=VMEM_LIMIT):
    return pltpu.CompilerParams(dimension_semantics=("arbitrary",) * ndim, vmem_limit_bytes=vmem)


def _sigmoid(x):
    return 1.0 / (1.0 + jnp.exp(-x))


SPLIT3 = "split3"


def _split(v):
    hi = v.astype(bf16)
    return hi, (v - hi.astype(f32)).astype(bf16)


def _dot(a, b, precision=None):
    if precision == SPLIT3:
        (ah, al), (bh, bl) = _split(a), _split(b)
        return _dot(ah, bh) + _dot(ah, bl) + _dot(al, bh)
    return jnp.dot(a, b, precision=precision, preferred_element_type=f32)


def _dot_nt(a, b, precision=None):
    return lax.dot_general(a, b, (((1,), (1,)), ((), ())), precision=precision, preferred_element_type=f32)


def _dot_tn(a, b):
    return lax.dot_general(a, b, (((0,), (0,)), ((), ())), preferred_element_type=f32)


def _bdot(a, b, precision=None):
    return lax.dot_general(a, b, (((2,), (1,)), ((0,), (0,))), precision=precision, preferred_element_type=f32)


def _adaln_kernel(cv_ref, w_ref, b_ref, o_ref):
    cv = cv_ref[...]
    o_ref[...] = _dot(cv * _sigmoid(cv), w_ref[...], HI) + b_ref[...]


def _adaln(cv, w, b):
    rows, d = cv.shape
    n = w.shape[1]
    tn = 1024
    return pl.pallas_call(
        _adaln_kernel,
        grid=(n // tn,),
        in_specs=[
            pl.BlockSpec((rows, d), lambda j: (0, 0)),
            pl.BlockSpec((d, tn), lambda j: (0, j)),
            pl.BlockSpec((1, tn), lambda j: (0, j)),
        ],
        out_specs=pl.BlockSpec((rows, tn), lambda j: (0, j)),
        out_shape=jax.ShapeDtypeStruct((rows, n), f32),
        compiler_params=_cparams(1),
        name="adaln",
    )(cv, w, b)


def _head_norm(acc, gain, ones):
    s2 = acc * acc
    hi = s2.astype(bf16)
    lo = (s2 - hi.astype(f32)).astype(bf16)
    ssum = _dot(hi, ones) + _dot(lo, ones)
    return acc * lax.rsqrt(ssum * (1.0 / HEAD_DIM) + RMS_EPS) * gain


def _inproj_kernel(n_norm, n_hm, n_u, n_g, x_ref, g_ref, sh_ref, sc_ref, w_ref, ones_ref, *rest):
    gains, rest = rest[:n_norm], list(rest[n_norm:])
    hm_ref = rest.pop(0)
    u_ref = rest.pop(0) if n_u else None
    gate_ref = rest.pop(0) if n_g else None
    h_scr = rest.pop(0)
    j = pl.program_id(1)

    @pl.when(j == 0)
    def _():
        xf = x_ref[...]
        ms = jnp.mean(xf * xf, axis=-1, keepdims=True)
        y = xf * lax.rsqrt(ms + RMS_EPS) * g_ref[...]
        h_scr[...] = (y * (1.0 + sc_ref[0]) + sh_ref[0]).astype(bf16)

    project = lambda: _dot(h_scr[...], w_ref[...])

    @pl.when(j < n_norm)
    def _():
        gain = gains[n_norm - 1][...]
        for t in range(n_norm - 2, -1, -1):
            gain = jnp.where(j == t, gains[t][...], gain)
        half = D_MODEL // 2
        for c0 in (0, half):
            acc = _dot(h_scr[...], w_ref[:, c0:c0 + half])
            for k in range(half // LANES):
                sl = slice(c0 + k * LANES, c0 + (k + 1) * LANES)
                hm_ref[sl.start // LANES] = _head_norm(acc[:, k * LANES:(k + 1) * LANES], gain[:, sl],
                                                       ones_ref[...]).astype(bf16)

    if n_hm > n_norm:

        @pl.when((j >= n_norm) & (j < n_hm))
        def _():
            acc = project()
            for hp in range(N_PAIRS):
                hm_ref[hp] = acc[:, hp * LANES:(hp + 1) * LANES].astype(bf16)

    if n_u:

        @pl.when((j >= n_hm) & (j < n_hm + n_u))
        def _():
            u_ref[...] = project().astype(bf16)

    if n_g:

        @pl.when(j >= n_hm + n_u)
        def _():
            half = D_MODEL // 2
            for c0 in (0, half):
                gate_ref[:, c0:c0 + half] = _sigmoid(_dot(h_scr[...], w_ref[:, c0:c0 + half])).astype(bf16)


def _inproj(x2, gain, shift, scale, w_bf, norm_gains, ones, n_hm, n_u, n_g, col0, tm, tiles_per_batch):
    n, d = x2.shape
    n_norm = len(norm_gains)
    out_shapes = [jax.ShapeDtypeStruct((n_hm * N_PAIRS, n, LANES), bf16)]
    out_specs = [pl.BlockSpec((N_PAIRS, tm, LANES), lambda i, j: (jnp.clip(j, 0, n_hm - 1), i, 0))]
    for j0, ncol in ((n_hm, n_u), (n_hm + n_u, n_g)):
        if ncol:
            out_shapes.append(jax.ShapeDtypeStruct((n, d * ncol), bf16))
            out_specs.append(pl.BlockSpec((tm, d), lambda i, j, j0=j0, ncol=ncol: (i, jnp.clip(j - j0, 0, ncol - 1))))
    vec = lambda: pl.BlockSpec((1, d), lambda i, j: (0, 0))
    mod = lambda: pl.BlockSpec((1, 1, d), lambda i, j: (i // tiles_per_batch, 0, 0))
    return pl.pallas_call(
        functools.partial(_inproj_kernel, n_norm, n_hm, n_u, n_g),
        grid=(n // tm, n_hm + n_u + n_g),
        in_specs=[
            pl.BlockSpec((tm, d), lambda i, j: (i, 0)),
            vec(), mod(), mod(),
            pl.BlockSpec((d, d), lambda i, j: (0, j + col0)),
            pl.BlockSpec(ones.shape, lambda i, j: (0, 0)),
        ] + [vec() for _ in norm_gains],
        out_specs=out_specs,
        out_shape=out_shapes,
        scratch_shapes=[pltpu.VMEM((tm, d), bf16)],
        compiler_params=_cparams(2),
        name="inproj",
    )(x2, gain, shift, scale, w_bf, ones, *norm_gains)


def _slab_bias_tables(rpb):
    n_drow = 2 * WIN_ROWS - 1
    c = np.arange(GRID_W)
    cs = np.clip(c - WIN_COLS // 2, 0, GRID_W - WIN_COLS)
    j = np.arange(GRID_W)
    col_ok = (j[None, :] >= cs[:, None]) & (j[None, :] < cs[:, None] + WIN_COLS)
    pad = GRID_W - WIN_COLS
    rp = jnp.pad(rpb.astype(f32), ((0, 0), (0, 0), (pad, pad)))
    bm = jnp.stack([rp[:, :, pad + WIN_COLS - 1 - ci:pad + WIN_COLS - 1 - ci + GRID_W] for ci in range(GRID_W)], axis=2)
    bm = jnp.where(col_ok[None, None], bm, NEG_BIG)
    return pl.pallas_call(
        functools.partial(_slab_bias_kernel, n_drow),
        grid=(N_HEADS,),
        in_specs=[pl.BlockSpec((1, n_drow, GRID_W, GRID_W), lambda h: (h, 0, 0, 0))],
        out_specs=pl.BlockSpec((1, N_SLAB_TYPES, 2 * GRID_W, SLAB_ROWS * GRID_W), lambda h: (h, 0, 0, 0)),
        out_shape=jax.ShapeDtypeStruct((N_HEADS, N_SLAB_TYPES, 2 * GRID_W, SLAB_ROWS * GRID_W), f32),
        compiler_params=_cparams(1),
        name="slabbias",
    )(bm)


_SLAB_SPEC = {0: [(3, 0, 8), (2, 1, 9)], 1: [(7, 0, 8), (6, 0, 8)], 2: [(5, 0, 8), (4, 0, 8)],
              3: [(1, 2, 10), (0, 2, 10)], 4: [(-1, 2, 10), (-2, 2, 10)]}


def _slab_bias_kernel(n_drow, bm_ref, o_ref):
    masked = jnp.full((GRID_W, GRID_W), NEG_BIG, f32)
    for t in range(N_SLAB_TYPES):
        for qi, (start, lo, hi) in enumerate(_SLAB_SPEC[t]):
            for kl in range(SLAB_ROWS):
                drow = start + kl
                ok = lo <= kl < hi and 0 <= drow < n_drow
                tile = bm_ref[0, drow] if ok else masked
                o_ref[0, t, qi * GRID_W:(qi + 1) * GRID_W, kl * GRID_W:(kl + 1) * GRID_W] = tile


ATTN_PAIRS = 4


def _attn_kernel(n_rows, q_ref, *refs):
    k_refs, v_refs = refs[:ATTN_PAIRS], refs[ATTN_PAIRS:2 * ATTN_PAIRS]
    kc_ref, vc_ref, sb_ref, o_ref = refs[2 * ATTN_PAIRS:]
    for g in range(ATTN_PAIRS):
        _attn_head_pair(n_rows, g, q_ref, k_refs[g], v_refs[g], kc_ref, vc_ref, sb_ref, o_ref)


def _attn_head_pair(n_rows, g, q_ref, k_ref, v_ref, kc_ref, vc_ref, sb_ref, o_ref):
    rb = pl.program_id(2)
    base = jnp.clip(QROWS * rb - WIN_ROWS // 2, 0, n_rows - KROWS)
    tq = q_ref.shape[1]
    pr = 2 * GRID_W
    slab = SLAB_ROWS * GRID_W
    lane = lax.broadcasted_iota(i32, (tq, LANES), 1)
    q = q_ref[g]
    zero = jnp.zeros_like(q)
    q2 = jnp.concatenate([jnp.where(lane < HEAD_DIM, q, zero), jnp.where(lane >= HEAD_DIM, q, zero)], axis=0)
    sc_all = _dot_nt(q2, kc_ref[g])
    pws, pcs, dens = [], [], []
    for ip in range(QROWS // 2):
        r0 = QROWS * rb + 2 * ip
        rs0 = jnp.clip(r0 - WIN_ROWS // 2, 0, n_rows - WIN_ROWS)
        s = jnp.minimum(rs0 - base, KROWS - SLAB_ROWS)
        typ = jnp.where(r0 < 2, 1, jnp.where(r0 < 4, 2, jnp.where(r0 == n_rows - 4, 3, jnp.where(r0 == n_rows - 2, 4, 0))))
        off = pl.multiple_of(s * GRID_W, GRID_W)
        kw = k_ref[0, pl.ds(off, slab), :]
        ra, rbb = slice(ip * pr, (ip + 1) * pr), slice(tq + ip * pr, tq + (ip + 1) * pr)
        qq = jnp.concatenate([q2[ra], q2[rbb]], axis=0)
        sw = _dot_nt(qq, kw) + jnp.concatenate([sb_ref[2 * g, typ], sb_ref[2 * g + 1, typ]], axis=0)
        sc = jnp.concatenate([sc_all[ra], sc_all[rbb]], axis=0)
        m = jnp.maximum(jnp.max(sw, axis=-1, keepdims=True), jnp.max(sc, axis=-1, keepdims=True))
        pw = jnp.exp(sw - m)
        pc = jnp.exp(sc - m)
        dens.append(jnp.sum(pw, axis=-1, keepdims=True) + jnp.sum(pc, axis=-1, keepdims=True))
        pws.append((pw.astype(bf16), off))
        pcs.append(pc.astype(bf16))
    oc = _dot(jnp.concatenate(pcs, axis=0), vc_ref[g])
    lane_p = lax.broadcasted_iota(i32, (pr, LANES), 1)
    for ip in range(QROWS // 2):
        pw, off = pws[ip]
        o = (_dot(pw, v_ref[0, pl.ds(off, slab), :]) + oc[ip * 2 * pr:(ip + 1) * 2 * pr]) / dens[ip]
        o_ref[g, ip * pr:(ip + 1) * pr, :] = jnp.where(lane_p < HEAD_DIM, o[:pr], o[pr:]).astype(bf16)


def _attention(qkv, ckv, sb, batch, seq):
    n_rows = seq // GRID_W
    assert n_rows % QROWS == 0 and n_rows >= KROWS
    nrb = n_rows // QROWS
    tq = QROWS * GRID_W
    tk = KROWS * GRID_W
    n_tok = batch * seq

    def kv_map(which, g):
        def index(b, hg, rb):
            base = jnp.clip(QROWS * rb - WIN_ROWS // 2, 0, n_rows - KROWS)
            return (which * N_PAIRS + hg * ATTN_PAIRS + g, (b * n_rows + base) * GRID_W, 0)
        return index

    kv_specs = lambda which: [pl.BlockSpec((pl.Element(1), pl.Element(tk), pl.Element(LANES)), kv_map(which, g))
                              for g in range(ATTN_PAIRS)]
    n_groups = N_PAIRS // ATTN_PAIRS
    ctx_spec = lambda which: pl.BlockSpec((ATTN_PAIRS, CTX_LEN, LANES),
                                          lambda b, hg, rb: (which * n_groups + hg, b, 0))
    return pl.pallas_call(
        functools.partial(_attn_kernel, n_rows),
        grid=(batch, n_groups, nrb),
        in_specs=[pl.BlockSpec((ATTN_PAIRS, tq, LANES), lambda b, hg, rb: (hg, b * nrb + rb, 0))]
        + kv_specs(1) + kv_specs(2) + [
            ctx_spec(0), ctx_spec(1),
            pl.BlockSpec((2 * ATTN_PAIRS, N_SLAB_TYPES, 2 * GRID_W, SLAB_ROWS * GRID_W), lambda b, hg, rb: (hg, 0, 0, 0)),
        ],
        out_specs=pl.BlockSpec((ATTN_PAIRS, tq, LANES), lambda b, hg, rb: (hg, b * nrb + rb, 0)),
        out_shape=jax.ShapeDtypeStruct((N_PAIRS, n_tok, LANES), bf16),
        compiler_params=_cparams(3),
        name="attn",
    )(qkv, *([qkv] * (2 * ATTN_PAIRS)), ckv, ckv, sb)


A_CHUNK = 8


def _filter_kernel(seq, w1t_ref, w1c_ref, w1s_ref, b1_ref, fr1_ref, w2t_ref, b2_ref, fr2_ref, w3t_ref, dl_ref,
                   fq_ref, o_ref):
    st = pl.program_id(0)
    npos = A_CHUNK * LANES
    n = st * npos + lax.broadcasted_iota(i32, (1, npos), 1)
    idx = jnp.where(n < seq, n, 2 * seq - n).astype(f32)
    t = idx / (seq - 1)
    ang = (2.0 * math.pi / seq) * idx
    fa = fq_ref[...] * ang
    z1 = w1t_ref[...] * t + _dot(w1c_ref[...], jnp.cos(fa), HI) + _dot(w1s_ref[...], -jnp.sin(fa), HI) + b1_ref[...]
    h1 = jnp.sin(fr1_ref[...] * z1)
    h2 = jnp.sin(fr2_ref[...] * (_dot(w2t_ref[...], h1, HI) + b2_ref[...]))
    filt = _dot(w3t_ref[0], h2, SPLIT3) * jnp.exp(-t * dl_ref[...])
    filt = jnp.where(n == seq, 0.0, filt)
    for al in range(A_CHUNK):
        o_ref[0, pl.ds(al, D_B, stride=A_CHUNK), :] = filt[:, al * LANES:(al + 1) * LANES]


def _from_chunks(blk, db):
    cat = lambda d: jnp.concatenate([blk[c, d * A_CHUNK:(d + 1) * A_CHUNK] for c in range(blk.shape[0])], axis=0)
    return jnp.stack([cat(d) for d in range(db)])


def _to_chunks(y):
    cat = lambda c: jnp.concatenate([y[d, c * A_CHUNK:(c + 1) * A_CHUNK] for d in range(y.shape[0])], axis=0)
    return jnp.stack([cat(c) for c in range(y.shape[1] // A_CHUNK)])


def _hyena_filter(seq, w1, b1, fr1, w2, b2, fr2, w3):
    na = 2 * seq // LANES
    steps = na // A_CHUNK
    fwd_steps = steps // 2
    col = lambda v: v.reshape(-1, 1).astype(f32)
    deltas = jnp.abs(jnp.linspace(math.log(DECAY_TARGET) / SLOW_DECAY_PCT, math.log(DECAY_TARGET) / FAST_DECAY_PCT,
                                  D_B, dtype=f32))
    freqs = jnp.linspace(1e-4, N_BANDS - 1, N_BANDS, dtype=f32)
    w3t = w3.astype(f32).T.reshape(2, D_B, FILT_HID)
    w1 = w1.astype(f32)
    args = (w1[0:1].T, w1[1:1 + N_BANDS].T, w1[1 + N_BANDS:].T, col(b1), col(fr1), w2.astype(f32).T, col(b2), col(fr2),
            w3t, col(deltas), col(freqs))
    full = lambda a: pl.BlockSpec(a.shape, lambda s: (0,) * a.ndim)
    in_specs = [full(a) for a in args]
    in_specs[8] = pl.BlockSpec((1, D_B, FILT_HID), lambda s: (s // fwd_steps, 0, 0))
    return pl.pallas_call(
        functools.partial(_filter_kernel, seq),
        grid=(steps,),
        in_specs=in_specs,
        out_specs=pl.BlockSpec((1, D_B * A_CHUNK, LANES), lambda s: (s, 0, 0)),
        out_shape=jax.ShapeDtypeStruct((steps, D_B * A_CHUNK, LANES), f32),
        compiler_params=_cparams(1),
        name="filt",
    )(*args)


def _dft_consts(seq):
    n = 2 * seq
    n1 = n // LANES
    a_used = seq // LANES
    k1 = np.arange(n1)[:, None]
    a = np.arange(n1)[None, :]
    ang1 = -2.0 * np.pi * ((k1 * a) % n1) / n1
    f1 = np.concatenate([np.cos(ang1), np.sin(ang1)], axis=0)
    b = np.arange(LANES)[None, :]
    angt = -2.0 * np.pi * ((k1 * b) % n) / n
    tw_r, tw_i = np.cos(angt), np.sin(angt)
    bb = np.arange(LANES)[:, None]
    k2 = np.arange(LANES)[None, :]
    angg = -2.0 * np.pi * ((bb * k2) % LANES) / LANES
    gr, gi = np.cos(angg), np.sin(angg)
    g_fwd = np.block([[gr, gi], [-gi, gr]])
    g_inv = np.block([[gr, -gi], [gi, gr]])
    f1_inv = np.concatenate([np.cos(ang1).T[:a_used], np.sin(ang1).T[:a_used]], axis=1)
    c = lambda x: jnp.asarray(x, f32)
    return dict(n1=n1, a_used=a_used, f1_full=c(f1), f1=c(f1[:, :a_used]), tw_r=c(tw_r), tw_i=c(tw_i), g_fwd=c(g_fwd),
                g_inv=c(g_inv), f1_inv=c(f1_inv))


def _pair_dot(m, x, precision=None):
    out = []
    for p in range(x.shape[0] // 2):
        y2 = _dot(m, jnp.concatenate([x[2 * p], x[2 * p + 1]], axis=1), precision)
        out += [y2[:, :LANES], y2[:, LANES:]]
    return jnp.stack(out)


def _fft_fwd(x, f1, tw_r, tw_i, g_fwd, precision, cast):
    db = x.shape[0]
    n1 = tw_r.shape[0]
    y = _pair_dot(cast(f1), cast(x), precision)
    yr, yi = y[:, :n1], y[:, n1:]
    ytr = yr * tw_r - yi * tw_i
    yti = yr * tw_i + yi * tw_r
    lhs = jnp.concatenate([ytr, yti], axis=-1).reshape(db * n1, 2 * LANES)
    return _dot(cast(lhs), cast(g_fwd), precision)


def _fftk_kernel(x_ref, f1_ref, twr_ref, twi_ref, g_ref, o_ref):
    db, n1 = o_ref.shape[0], twr_ref.shape[0]
    x = _from_chunks(x_ref[...], db)
    z = _fft_fwd(x, f1_ref[...], twr_ref[...], twi_ref[...], g_ref[...], SPLIT3, lambda v: v)
    o_ref[...] = z.reshape(db, n1, 2 * LANES)


def _filter_spectrum(kf, dc):
    n_chunks = kf.shape[0]
    d, n1 = kf.shape[1] // A_CHUNK, n_chunks * A_CHUNK
    db = 16
    full = lambda a: pl.BlockSpec(a.shape, lambda i: (0,) * a.ndim)
    consts = (dc["f1_full"], dc["tw_r"], dc["tw_i"], dc["g_fwd"])
    return pl.pallas_call(
        _fftk_kernel,
        grid=(d // db,),
        in_specs=[pl.BlockSpec((n_chunks, db * A_CHUNK, LANES), lambda i: (0, i, 0))] + [full(a) for a in consts],
        out_specs=pl.BlockSpec((db, n1, 2 * LANES), lambda i: (i, 0, 0)),
        out_shape=jax.ShapeDtypeStruct((d, n1, 2 * LANES), f32),
        compiler_params=_cparams(1),
        name="fftk",
    )(kf, *consts)


def _hpre_kernel(n_chunks, u_ref, up_ref, un_ref, cw_ref, cb_ref, zt_ref, x0_ref):
    ac = pl.program_id(1)
    tm = u_ref.shape[0]
    has_prev = (ac > 0).astype(f32)
    has_next = (ac < n_chunks - 1).astype(f32)
    row = lax.broadcasted_iota(i32, (tm, LANES), 0)
    halo = up_ref.shape[0]

    def conv(c0):
        sl = slice(c0, c0 + LANES)
        u = u_ref[:, sl].astype(f32)
        prev = up_ref[halo - 1:halo, sl].astype(f32) * has_prev
        nxt = un_ref[0:1, sl].astype(f32) * has_next
        um = jnp.where(row == 0, prev, pltpu.roll(u, 1, 0))
        upl = jnp.where(row == tm - 1, nxt, pltpu.roll(u, tm - 1, 0))
        return um * cw_ref[0:1, sl] + u * cw_ref[1:2, sl] + upl * cw_ref[2:3, sl] + cb_ref[:, sl]

    for dt in range(D_B // LANES):
        v = conv(dt * LANES)
        x1 = conv(D_B + dt * LANES)
        x0_ref[:, dt * LANES:(dt + 1) * LANES] = conv(2 * D_B + dt * LANES).astype(bf16)
        z = v * x1
        for al in range(A_CHUNK):
            rows = pl.ds(dt * LANES * A_CHUNK + al, LANES, stride=A_CHUNK)
            zt_ref[0, 0, rows, :] = z[al * LANES:(al + 1) * LANES, :].T


def _hyena_pre(u, conv_w, conv_b, batch, seq):
    n = u.shape[0]
    tm = A_CHUNK * LANES
    n_chunks = seq // tm
    halo = 16
    hb = tm // halo
    cw = jnp.concatenate([conv_w.astype(f32), jnp.zeros((8 - conv_w.shape[0], conv_w.shape[1]), f32)], axis=0)
    return pl.pallas_call(
        functools.partial(_hpre_kernel, n_chunks),
        grid=(batch, n_chunks),
        in_specs=[
            pl.BlockSpec((tm, 3 * D_B), lambda b, c: (b * n_chunks + c, 0)),
            pl.BlockSpec((halo, 3 * D_B), lambda b, c: (jnp.maximum((b * n_chunks + c) * hb - 1, 0), 0)),
            pl.BlockSpec((halo, 3 * D_B), lambda b, c: (jnp.minimum((b * n_chunks + c + 1) * hb, n // halo - 1), 0)),
            pl.BlockSpec((8, 3 * D_B), lambda b, c: (0, 0)),
            pl.BlockSpec((1, 3 * D_B), lambda b, c: (0, 0)),
        ],
        out_specs=[
            pl.BlockSpec((1, 1, D_B * A_CHUNK, LANES), lambda b, c: (b, c, 0, 0)),
            pl.BlockSpec((tm, D_B), lambda b, c: (b * n_chunks + c, 0)),
        ],
        out_shape=[
            jax.ShapeDtypeStruct((batch, n_chunks, D_B * A_CHUNK, LANES), f32),
            jax.ShapeDtypeStruct((n, D_B), bf16),
        ],
        compiler_params=_cparams(2),
        name="hpre",
    )(u, u, u, cw, conv_b.reshape(1, -1).astype(f32))


def _fftconv_kernel(z_ref, kh_ref, skip_ref, f1_ref, twr_ref, twi_ref, gf_ref, gi_ref, f1i_ref, y_ref):
    db = skip_ref.shape[0]
    n1 = twr_ref.shape[0]
    tw_r, tw_i = twr_ref[...], twi_ref[...]
    cast = lambda v: v.astype(bf16)
    z = _from_chunks(z_ref[0], db)
    spec = _fft_fwd(z, f1_ref[...], tw_r, tw_i, gf_ref[...], None, cast)
    kh = kh_ref[...].reshape(db * n1, 2 * LANES)
    sr, si = spec[:, :LANES], spec[:, LANES:]
    kr, ki = kh[:, :LANES], kh[:, LANES:]
    prod = jnp.concatenate([sr * kr - si * ki, sr * ki + si * kr], axis=-1)
    q = _dot(cast(prod), cast(gi_ref[...])).reshape(db, n1, 2 * LANES)
    qr, qi = q[:, :, :LANES], q[:, :, LANES:]
    rhs = jnp.concatenate([qr * tw_r + qi * tw_i, qi * tw_r - qr * tw_i], axis=1)
    y = _pair_dot(cast(f1i_ref[...]), cast(rhs)) * (1.0 / (n1 * LANES))
    y_ref[0] = _to_chunks(y + z * skip_ref[...])


def _fftconv(zt, khat, skip, dc):
    batch, n_chunks, rows, _ = zt.shape
    d = rows // A_CHUNK
    n1 = dc["n1"]
    db = 64
    consts = (dc["f1"], dc["tw_r"], dc["tw_i"], dc["g_fwd"], dc["g_inv"], dc["f1_inv"])
    full = lambda a: pl.BlockSpec(a.shape, lambda i, b: (0,) * a.ndim)
    blk = lambda: pl.BlockSpec((1, n_chunks, db * A_CHUNK, LANES), lambda i, b: (b, 0, i, 0))
    return pl.pallas_call(
        _fftconv_kernel,
        grid=(d // db, batch),
        in_specs=[
            blk(),
            pl.BlockSpec((db, n1, 2 * LANES), lambda i, b: (i, 0, 0)),
            pl.BlockSpec((db, 1, 1), lambda i, b: (i, 0, 0)),
        ] + [full(a) for a in consts],
        out_specs=blk(),
        out_shape=jax.ShapeDtypeStruct(zt.shape, f32),
        compiler_params=_cparams(2),
        name="fftconv",
    )(zt, khat, skip.reshape(d, 1, 1).astype(f32), *consts)


IDX_ROWS = 8


def _merge_kernel(oa_ref, yc_ref, x0_ref, ga_ref, gb_ref, x_ref, g1_ref, wba_ref, wbb_ref, wout_ref, n2g_ref, sh2_ref,
                  sc2_ref, wrt_ref, br_ref, x1_ref, h2_ref, idx_ref, wt_ref, ob_scr):
    a_tile = x0_ref.shape[0] // LANES
    part = pl.program_id(0) % (A_CHUNK // a_tile)
    for dt in range(D_B // LANES):
        for al in range(a_tile):
            rows = slice(al * LANES, (al + 1) * LANES)
            cols = slice(dt * LANES, (dt + 1) * LANES)
            src = pl.ds(dt * LANES * A_CHUNK + part * a_tile + al, LANES, stride=A_CHUNK)
            ob_scr[rows, cols] = (x0_ref[rows, cols].astype(f32) * yc_ref[0, 0, src, :].T).astype(bf16)
    oa = jnp.concatenate([oa_ref[hp] for hp in range(N_PAIRS)], axis=1)
    m = ga_ref[...].astype(f32) * _dot(oa, wba_ref[...]) + gb_ref[...].astype(f32) * _dot(ob_scr[...], wbb_ref[...])
    x1 = x_ref[...] + g1_ref[0] * _dot(m.astype(bf16), wout_ref[...])
    x1_ref[...] = x1
    ms = jnp.mean(x1 * x1, axis=-1, keepdims=True)
    h2 = (x1 * lax.rsqrt(ms + RMS_EPS) * n2g_ref[...]) * (1.0 + sc2_ref[0]) + sh2_ref[0]
    h_hi = h2.astype(bf16)
    h2_ref[...] = h_hi
    h_lo = (h2 - h_hi.astype(f32)).astype(bf16)
    wr = wrt_ref[...]
    w_hi = wr.astype(bf16)
    w_lo = (wr - w_hi.astype(f32)).astype(bf16)
    logits = _dot_nt(w_hi, h_hi) + _dot_nt(w_hi, h_lo) + _dot_nt(w_lo, h_hi) + br_ref[...]
    eio = lax.broadcasted_iota(i32, logits.shape, 0)
    vals = logits
    idxs, tops = [], []
    for _ in range(TOP_K):
        mx = jnp.max(vals, axis=0, keepdims=True)
        ix = jnp.min(jnp.where(vals == mx, eio, N_EXPERTS), axis=0, keepdims=True)
        idxs.append(ix)
        tops.append(mx)
        vals = jnp.where(eio == ix, -jnp.inf, vals)
    ex = [jnp.exp(v - tops[0]) for v in tops]
    den = ex[0] + ex[1] + ex[2] + ex[3]
    idx_ref[...] = jnp.concatenate(idxs + [jnp.full_like(idxs[0], -1)] * (IDX_ROWS - TOP_K), axis=0)
    wt_ref[...] = jnp.concatenate([e / den for e in ex] + [jnp.zeros_like(den)] * (IDX_ROWS - TOP_K), axis=0)


def _merge(oa3, yc, x0, gates, x2, g1, wba, wbb, wout, n2g, sh2, sc2, wrt, br, tm, tiles_per_batch):
    n, d = x2.shape
    parts = A_CHUNK * LANES // tm
    tok = lambda col=0: pl.BlockSpec((tm, d), lambda i: (i, col))
    mod = lambda: pl.BlockSpec((1, 1, d), lambda i: (i // tiles_per_batch, 0, 0))
    wsp = lambda: pl.BlockSpec((d, d), lambda i: (0, 0))
    ga, gb = gates, gates
    return pl.pallas_call(
        _merge_kernel,
        grid=(n // tm,),
        in_specs=[
            pl.BlockSpec((N_PAIRS, tm, LANES), lambda i: (0, i, 0)),
            pl.BlockSpec((1, 1, D_B * A_CHUNK, LANES),
                         lambda i: (i // tiles_per_batch, (i % tiles_per_batch) // parts, 0, 0)),
            tok(), tok(0), tok(1), tok(), mod(), wsp(), wsp(), wsp(),
            pl.BlockSpec((1, d), lambda i: (0, 0)), mod(), mod(),
            pl.BlockSpec((N_EXPERTS, d), lambda i: (0, 0)),
            pl.BlockSpec((N_EXPERTS, 1), lambda i: (0, 0)),
        ],
        out_specs=[tok(), tok(), pl.BlockSpec((IDX_ROWS, tm), lambda i: (0, i)), pl.BlockSpec((IDX_ROWS, tm), lambda i: (0, i))],
        out_shape=[
            jax.ShapeDtypeStruct((n, d), f32),
            jax.ShapeDtypeStruct((n, d), bf16),
            jax.ShapeDtypeStruct((IDX_ROWS, n), i32),
            jax.ShapeDtypeStruct((IDX_ROWS, n), f32),
        ],
        scratch_shapes=[pltpu.VMEM((tm, d), bf16)],
        compiler_params=_cparams(1),
        name="merge",
    )(oa3, yc, x0, ga, gb, x2, g1, wba, wbb, wout, n2g, sh2, sc2, wrt, br)


MOE_SUB = 1024
MOE_NSUB = 2
MOE_TILE = MOE_SUB * MOE_NSUB
MOE_ROWS = 160
MOE_CAP = 12 * MOE_ROWS


def _moe_kernel(h2_ref, idx_ref, wt_ref, wgu_ref, bgu_ref, wd_ref, bd_ref, x1_ref, g2_ref, y_ref, rank_scr, msk_scr,
                wte_scr, o_scr, p_scr, st_ref, cnt_ref):
    e = pl.program_id(1)
    ts = MOE_SUB
    last = e == N_EXPERTS - 1

    @pl.when(e == 0)
    def _():
        y_ref[...] = jnp.zeros_like(y_ref)
        o_scr[...] = jnp.zeros_like(o_scr)
        p_scr[...] = jnp.zeros_like(p_scr)
        eio = lax.broadcasted_iota(i32, (N_EXPERTS, ts), 0)
        lane = lax.broadcasted_iota(i32, (N_EXPERTS, ts), 1)

        for s in range(MOE_NSUB):
            idx = idx_ref[s]
            w = wt_ref[s]
            msk = jnp.zeros((N_EXPERTS, ts), f32)
            wte = jnp.zeros((N_EXPERTS, ts), f32)
            for k in range(TOP_K):
                hit = idx[k:k + 1, :] == eio
                msk = msk + hit.astype(f32)
                wte = wte + jnp.where(hit, w[k:k + 1, :], 0.0)
            csum = msk
            sh = 1
            while sh < ts:
                csum = csum + jnp.where(lane >= sh, pltpu.roll(csum, sh, 1), 0.0)
                sh *= 2
            rank_scr[s] = (csum - msk).astype(i32)
            msk_scr[s] = msk
            wte_scr[s] = wte
            st_ref[s] = 0
            for ex in range(N_EXPERTS):
                cnt_ref[s * N_EXPERTS + ex] = jnp.sum(csum[ex:ex + 1, ts - 1:ts]).astype(i32)

    subs = range(MOE_NSUB)
    toks = [slice(s * ts, (s + 1) * ts) for s in subs]
    ranks = [rank_scr[s, pl.ds(e, 1), :] for s in subs]
    sels = [msk_scr[s, pl.ds(e, 1), :] > 0.0 for s in subs]
    wtes = [wte_scr[s, pl.ds(e, 1), :] for s in subs]
    n_chunks = [(cnt_ref[s * N_EXPERTS + e] + MOE_ROWS - 1) // MOE_ROWS for s in subs]
    max_chunks = functools.reduce(jnp.maximum, n_chunks)

    def combine(s):
        y_ref[toks[s], :] += _dot_tn(p_scr[s], o_scr[s])
        o_scr[s] = jnp.zeros((MOE_CAP, D_MODEL), bf16)
        st_ref[s] = 0

    def trip(jc, carry):
        live = [jc < n for n in n_chunks]
        for s in subs:
            pl.when((live[s] & (st_ref[s] + MOE_ROWS > MOE_CAP)) | (jc >= max_chunks))(functools.partial(combine, s))

        @pl.when(jc < max_chunks)
        def _():
            jio = lax.broadcasted_iota(i32, (MOE_ROWS, ts), 0)
            hits = [((ranks[s] - jc * MOE_ROWS) == jio) & sels[s] for s in subs]
            onehots = [h.astype(bf16) for h in hits]
            xg = jnp.concatenate([_dot(onehots[s], h2_ref[toks[s], :]).astype(bf16) for s in subs], axis=0)
            gu = _dot(xg, wgu_ref[0]) + bgu_ref[0]
            gate = jnp.minimum(gu[:, :D_FF], SWIGLU_LIMIT)
            up = jnp.clip(gu[:, D_FF:], -SWIGLU_LIMIT, SWIGLU_LIMIT)
            act = (up + 1.0) * gate * _sigmoid(SWIGLU_ALPHA * gate)
            o = _dot(act.astype(bf16), wd_ref[0]) + bd_ref[0]
            for s in subs:

                @pl.when(live[s])
                def _(s=s):
                    fill = pl.multiple_of(st_ref[s], 16)
                    wrow = jnp.sum(jnp.where(hits[s], wtes[s], 0.0), axis=1, keepdims=True)
                    o_scr[s, pl.ds(fill, MOE_ROWS), :] = (o[s * MOE_ROWS:(s + 1) * MOE_ROWS] * wrow).astype(bf16)
                    p_scr[s, pl.ds(fill, MOE_ROWS), :] = onehots[s]
                    st_ref[s] = fill + MOE_ROWS

        return carry

    lax.fori_loop(0, max_chunks + last.astype(i32), trip, 0)

    @pl.when(last)
    def _():
        y_ref[...] = x1_ref[...] + g2_ref[0] * y_ref[...]


def _moe(h2, idx, wt, wgu, bgu, wd, bd, x1, g2, tiles_per_batch):
    n, d = h2.shape
    t = MOE_TILE
    once = dict(pipeline_mode=pl.Buffered(1))
    per_sub = lambda a: jnp.transpose(a.reshape(IDX_ROWS, n // MOE_SUB, MOE_SUB), (1, 0, 2))
    route_spec = lambda: pl.BlockSpec((MOE_NSUB, IDX_ROWS, MOE_SUB), lambda i, e: (i, 0, 0), **once)
    return pl.pallas_call(
        _moe_kernel,
        grid=(n // t, N_EXPERTS),
        in_specs=[
            pl.BlockSpec((t, d), lambda i, e: (i, 0), **once),
            route_spec(), route_spec(),
            pl.BlockSpec((1, d, 2 * D_FF), lambda i, e: (e, 0, 0)),
            pl.BlockSpec((1, 1, 2 * D_FF), lambda i, e: (e, 0, 0)),
            pl.BlockSpec((1, D_FF, d), lambda i, e: (e, 0, 0)),
            pl.BlockSpec((1, 1, d), lambda i, e: (e, 0, 0)),
            pl.BlockSpec((t, d), lambda i, e: (i, 0), **once),
            pl.BlockSpec((1, 1, d), lambda i, e: (i // tiles_per_batch, 0, 0)),
        ],
        out_specs=pl.BlockSpec((t, d), lambda i, e: (i, 0), **once),
        out_shape=jax.ShapeDtypeStruct((n, d), f32),
        scratch_shapes=[
            pltpu.VMEM((MOE_NSUB, N_EXPERTS, MOE_SUB), i32),
            pltpu.VMEM((MOE_NSUB, N_EXPERTS, MOE_SUB), f32),
            pltpu.VMEM((MOE_NSUB, N_EXPERTS, MOE_SUB), f32),
            pltpu.VMEM((MOE_NSUB, MOE_CAP, d), bf16),
            pltpu.VMEM((MOE_NSUB, MOE_CAP, MOE_SUB), bf16),
            pltpu.SMEM((MOE_NSUB,), i32),
            pltpu.SMEM((MOE_NSUB * N_EXPERTS,), i32),
        ],
        compiler_params=_cparams(2),
        name="moe",
    )(h2, per_sub(idx), per_sub(wt), wgu, bgu, wd, bd, x1, g2)


def _layer(x, c, ctx, c_ctx, w_ada, b_ada, norm1_g, norm2_g, w_in, q_norm_g, k_norm_g, rpb, conv_w, conv_b, filt_w1,
           filt_b1, filt_freq1, filt_w2, filt_b2, filt_freq2, filt_w3, hyena_skip, w_branch_a, w_branch_b, w_out,
           w_router, b_router, w_gate_up, b_gate_up, w_down, b_down):
    batch, seq, d = x.shape
    n = batch * seq
    n_ctx = ctx.shape[1]
    assert d == D_MODEL and n_ctx == CTX_LEN and seq % (A_CHUNK * LANES) == 0 and seq % MOE_TILE == 0

    rows = 8 * ((batch + 1 + 7) // 8)
    cv = jnp.zeros((rows, d), f32).at[:batch].set(c).at[batch].set(c_ctx)
    mod = _adaln(cv, w_ada, b_ada.reshape(1, -1))
    sh1, sc1, g1, sh2, sc2, g2 = [mod[:batch, i * d:(i + 1) * d].reshape(batch, 1, d) for i in range(N_MOD)]
    modc = jnp.broadcast_to(mod[batch:batch + 1], (batch, N_MOD * d))
    sh1c, sc1c = modc[:, :d].reshape(batch, 1, d), modc[:, d:2 * d].reshape(batch, 1, d)

    w_in_bf = w_in.astype(bf16)
    ones = jnp.asarray(np.kron(np.eye(2), np.ones((HEAD_DIM, HEAD_DIM))), bf16)
    qg = (jnp.tile(q_norm_g.astype(f32), N_HEADS) * (HEAD_DIM ** -0.5)).reshape(1, d)
    kg = jnp.tile(k_norm_g.astype(f32), N_HEADS).reshape(1, d)
    n1g = norm1_g.reshape(1, d)

    (ckv,) = _inproj(ctx.reshape(batch * n_ctx, d), n1g, sh1c, sc1c, w_in_bf, (kg,), ones, 2, 0, 0, 1, n_ctx, 1)
    tm = 512
    x2 = x.reshape(n, d)
    tm_in = 1024
    qkv, u, gates = _inproj(x2, n1g, sh1, sc1, w_in_bf, (qg, kg), ones, 3, 3, 2, 0, tm_in, seq // tm_in)

    oa3 = _attention(qkv, ckv, _slab_bias_tables(rpb), batch, seq)

    dc = _dft_consts(seq)
    kf = _hyena_filter(seq, filt_w1, filt_b1, filt_freq1, filt_w2, filt_b2, filt_freq2, filt_w3)
    khat = _filter_spectrum(kf, dc)
    zt, x0 = _hyena_pre(u, conv_w, conv_b, batch, seq)
    yc = _fftconv(zt, khat, hyena_skip, dc)

    x1, h2, idx, wt = _merge(oa3, yc, x0, gates, x2, g1, w_branch_a.astype(bf16), w_branch_b.astype(bf16),
                             w_out.astype(bf16), norm2_g.reshape(1, d), sh2, sc2, w_router.astype(f32).T,
                             b_router.reshape(-1, 1).astype(f32), tm, seq // tm)
    out = _moe(h2, idx, wt, w_gate_up.astype(bf16), b_gate_up.reshape(N_EXPERTS, 1, -1).astype(f32),
               w_down.astype(bf16), b_down.reshape(N_EXPERTS, 1, -1).astype(f32), x1, g2, seq // MOE_TILE)
    return out.reshape(batch, seq, d)


def kernel(x, c, ctx, c_ctx, w_ada, b_ada, norm1_g, norm2_g, w_in, q_norm_g, k_norm_g, rpb, conv_w, conv_b, filt_w1, filt_b1, filt_freq1, filt_w2, filt_b2, filt_freq2, filt_w3, hyena_skip, w_branch_a, w_branch_b, w_out, w_router, b_router, w_gate_up, b_gate_up, w_down, b_down):
    assert w_ada.shape[0] == 1, "single-layer stack"
    return _layer(x, c, ctx, c_ctx, w_ada[0], b_ada[0], norm1_g[0], norm2_g[0], w_in[0], q_norm_g[0], k_norm_g[0],
                  rpb[0], conv_w[0], conv_b[0], filt_w1[0], filt_b1[0], filt_freq1[0], filt_w2[0], filt_b2[0],
                  filt_freq2[0], filt_w3[0], hyena_skip[0], w_branch_a[0], w_branch_b[0], w_out[0], w_router[0],
                  b_router[0], w_gate_up[0], b_gate_up[0], w_down[0], b_down[0])
```

```python
import functools
import math

import numpy as np
import jax
import jax.numpy as jnp
from jax import lax
from jax.experimental import pallas as pl
from jax.experimental.pallas import tpu as pltpu

f32, bf16, i32 = jnp.float32, jnp.bfloat16, jnp.int32
HI = lax.Precision.HIGHEST

LANES = 128
D_MODEL = 1024
N_HEADS = 16
HEAD_DIM = 64
N_PAIRS = N_HEADS // 2
GRID_W = 64
WIN_ROWS = 8
WIN_COLS = 16
CTX_LEN = 256
N_MOD = 6
RMS_EPS = 1e-6
D_B = 1024
EMB_DIM = 33
FILT_HID = 64
N_BANDS = (EMB_DIM - 1) // 2
DECAY_TARGET = 1e-2
FAST_DECAY_PCT = 0.3
SLOW_DECAY_PCT = 1.5
N_EXPERTS = 32
TOP_K = 4
D_FF = 1024
SWIGLU_LIMIT = 7.0
SWIGLU_ALPHA = 1.702
NEG_BIG = -1e30

QROWS = 8
KROWS = 16
SLAB_ROWS = 10
N_SLAB_TYPES = 5
VMEM_LIMIT = 56 * 1024 * 1024


def _cparams(ndim, vmem=VMEM_LIMIT):
    return pltpu.CompilerParams(dimension_semantics=("arbitrary",) * ndim, vmem_limit_bytes=vmem)


def _sigmoid(x):
    return 1.0 / (1.0 + jnp.exp(-x))


SPLIT3 = "split3"


def _split(v):
    hi = v.astype(bf16)
    return hi, (v - hi.astype(f32)).astype(bf16)


def _dot(a, b, precision=None):
    if precision == SPLIT3:
        (ah, al), (bh, bl) = _split(a), _split(b)
        return _dot(ah, bh) + _dot(ah, bl) + _dot(al, bh)
    return jnp.dot(a, b, precision=precision, preferred_element_type=f32)


def _dot_nt(a, b, precision=None):
    return lax.dot_general(a, b, (((1,), (1,)), ((), ())), precision=precision, preferred_element_type=f32)


def _dot_tn(a, b):
    return lax.dot_general(a, b, (((0,), (0,)), ((), ())), preferred_element_type=f32)


def _bdot(a, b, precision=None):
    return lax.dot_general(a, b, (((2,), (1,)), ((0,), (0,))), precision=precision, preferred_element_type=f32)


def _adaln_kernel(cv_ref, w_ref, b_ref, o_ref):
    cv = cv_ref[...]
    o_ref[...] = _dot(cv * _sigmoid(cv), w_ref[...], HI) + b_ref[...]


def _adaln(cv, w, b):
    rows, d = cv.shape
    n = w.shape[1]
    tn = 1024
    return pl.pallas_call(
        _adaln_kernel,
        grid=(n // tn,),
        in_specs=[
            pl.BlockSpec((rows, d), lambda j: (0, 0)),
            pl.BlockSpec((d, tn), lambda j: (0, j)),
            pl.BlockSpec((1, tn), lambda j: (0, j)),
        ],
        out_specs=pl.BlockSpec((rows, tn), lambda j: (0, j)),
        out_shape=jax.ShapeDtypeStruct((rows, n), f32),
        compiler_params=_cparams(1),
        name="adaln",
    )(cv, w, b)


def _head_norm(acc, gain, ones):
    s2 = acc * acc
    hi = s2.astype(bf16)
    lo = (s2 - hi.astype(f32)).astype(bf16)
    ssum = _dot(hi, ones) + _dot(lo, ones)
    return acc * lax.rsqrt(ssum * (1.0 / HEAD_DIM) + RMS_EPS) * gain


def _inproj_kernel(n_norm, n_hm, n_u, n_g, x_ref, g_ref, sh_ref, sc_ref, w_ref, ones_ref, *rest):
    gains, rest = rest[:n_norm], list(rest[n_norm:])
    hm_ref = rest.pop(0)
    u_ref = rest.pop(0) if n_u else None
    gate_ref = rest.pop(0) if n_g else None
    h_scr = rest.pop(0)
    j = pl.program_id(1)

    @pl.when(j == 0)
    def _():
        xf = x_ref[...]
        ms = jnp.mean(xf * xf, axis=-1, keepdims=True)
        y = xf * lax.rsqrt(ms + RMS_EPS) * g_ref[...]
        h_scr[...] = (y * (1.0 + sc_ref[0]) + sh_ref[0]).astype(bf16)

    project = lambda: _dot(h_scr[...], w_ref[...])

    @pl.when(j < n_norm)
    def _():
        gain = gains[n_norm - 1][...]
        for t in range(n_norm - 2, -1, -1):
            gain = jnp.where(j == t, gains[t][...], gain)
        half = D_MODEL // 2
        for c0 in (0, half):
            acc = _dot(h_scr[...], w_ref[:, c0:c0 + half])
            for k in range(half // LANES):
                sl = slice(c0 + k * LANES, c0 + (k + 1) * LANES)
                hm_ref[sl.start // LANES] = _head_norm(acc[:, k * LANES:(k + 1) * LANES], gain[:, sl],
                                                       ones_ref[...]).astype(bf16)

    if n_hm > n_norm:

        @pl.when((j >= n_norm) & (j < n_hm))
        def _():
            acc = project()
            for hp in range(N_PAIRS):
                hm_ref[hp] = acc[:, hp * LANES:(hp + 1) * LANES].astype(bf16)

    if n_u:

        @pl.when((j >= n_hm) & (j < n_hm + n_u))
        def _():
            u_ref[...] = project().astype(bf16)

    if n_g:

        @pl.when(j >= n_hm + n_u)
        def _():
            half = D_MODEL // 2
            for c0 in (0, half):
                gate_ref[:, c0:c0 + half] = _sigmoid(_dot(h_scr[...], w_ref[:, c0:c0 + half])).astype(bf16)


def _inproj(x2, gain, shift, scale, w_bf, norm_gains, ones, n_hm, n_u, n_g, col0, tm, tiles_per_batch):
    n, d = x2.shape
    n_norm = len(norm_gains)
    out_shapes = [jax.ShapeDtypeStruct((n_hm * N_PAIRS, n, LANES), bf16)]
    out_specs = [pl.BlockSpec((N_PAIRS, tm, LANES), lambda i, j: (jnp.clip(j, 0, n_hm - 1), i, 0))]
    for j0, ncol in ((n_hm, n_u), (n_hm + n_u, n_g)):
        if ncol:
            out_shapes.append(jax.ShapeDtypeStruct((n, d * ncol), bf16))
            out_specs.append(pl.BlockSpec((tm, d), lambda i, j, j0=j0, ncol=ncol: (i, jnp.clip(j - j0, 0, ncol - 1))))
    vec = lambda: pl.BlockSpec((1, d), lambda i, j: (0, 0))
    mod = lambda: pl.BlockSpec((1, 1, d), lambda i, j: (i // tiles_per_batch, 0, 0))
    return pl.pallas_call(
        functools.partial(_inproj_kernel, n_norm, n_hm, n_u, n_g),
        grid=(n // tm, n_hm + n_u + n_g),
        in_specs=[
            pl.BlockSpec((tm, d), lambda i, j: (i, 0)),
            vec(), mod(), mod(),
            pl.BlockSpec((d, d), lambda i, j: (0, j + col0)),
            pl.BlockSpec(ones.shape, lambda i, j: (0, 0)),
        ] + [vec() for _ in norm_gains],
        out_specs=out_specs,
        out_shape=out_shapes,
        scratch_shapes=[pltpu.VMEM((tm, d), bf16)],
        compiler_params=_cparams(2),
        name="inproj",
    )(x2, gain, shift, scale, w_bf, ones, *norm_gains)


def _slab_bias_tables(rpb):
    n_drow = 2 * WIN_ROWS - 1
    c = np.arange(GRID_W)
    cs = np.clip(c - WIN_COLS // 2, 0, GRID_W - WIN_COLS)
    j = np.arange(GRID_W)
    col_ok = (j[None, :] >= cs[:, None]) & (j[None, :] < cs[:, None] + WIN_COLS)
    pad = GRID_W - WIN_COLS
    rp = jnp.pad(rpb.astype(f32), ((0, 0), (0, 0), (pad, pad)))
    bm = jnp.stack([rp[:, :, pad + WIN_COLS - 1 - ci:pad + WIN_COLS - 1 - ci + GRID_W] for ci in range(GRID_W)], axis=2)
    bm = jnp.where(col_ok[None, None], bm, NEG_BIG)
    return pl.pallas_call(
        functools.partial(_slab_bias_kernel, n_drow),
        grid=(N_HEADS,),
        in_specs=[pl.BlockSpec((1, n_drow, GRID_W, GRID_W), lambda h: (h, 0, 0, 0))],
        out_specs=pl.BlockSpec((1, N_SLAB_TYPES, 2 * GRID_W, SLAB_ROWS * GRID_W), lambda h: (h, 0, 0, 0)),
        out_shape=jax.ShapeDtypeStruct((N_HEADS, N_SLAB_TYPES, 2 * GRID_W, SLAB_ROWS * GRID_W), f32),
        compiler_params=_cparams(1),
        name="slabbias",
    )(bm)


_SLAB_SPEC = {0: [(3, 0, 8), (2, 1, 9)], 1: [(7, 0, 8), (6, 0, 8)], 2: [(5, 0, 8), (4, 0, 8)],
              3: [(1, 2, 10), (0, 2, 10)], 4: [(-1, 2, 10), (-2, 2, 10)]}


def _slab_bias_kernel(n_drow, bm_ref, o_ref):
    masked = jnp.full((GRID_W, GRID_W), NEG_BIG, f32)
    for t in range(N_SLAB_TYPES):
        for qi, (start, lo, hi) in enumerate(_SLAB_SPEC[t]):
            for kl in range(SLAB_ROWS):
                drow = start + kl
                ok = lo <= kl < hi and 0 <= drow < n_drow
                tile = bm_ref[0, drow] if ok else masked
                o_ref[0, t, qi * GRID_W:(qi + 1) * GRID_W, kl * GRID_W:(kl + 1) * GRID_W] = tile


ATTN_PAIRS = 4


def _attn_kernel(n_rows, q_ref, *refs):
    k_refs, v_refs = refs[:ATTN_PAIRS], refs[ATTN_PAIRS:2 * ATTN_PAIRS]
    kc_ref, vc_ref, sb_ref, o_ref = refs[2 * ATTN_PAIRS:]
    for g in range(ATTN_PAIRS):
        _attn_head_pair(n_rows, g, q_ref, k_refs[g], v_refs[g], kc_ref, vc_ref, sb_ref, o_ref)


def _attn_head_pair(n_rows, g, q_ref, k_ref, v_ref, kc_ref, vc_ref, sb_ref, o_ref):
    rb = pl.program_id(2)
    base = jnp.clip(QROWS * rb - WIN_ROWS // 2, 0, n_rows - KROWS)
    tq = q_ref.shape[1]
    pr = 2 * GRID_W
    slab = SLAB_ROWS * GRID_W
    lane = lax.broadcasted_iota(i32, (tq, LANES), 1)
    q = q_ref[g]
    zero = jnp.zeros_like(q)
    q2 = jnp.concatenate([jnp.where(lane < HEAD_DIM, q, zero), jnp.where(lane >= HEAD_DIM, q, zero)], axis=0)
    sc_all = _dot_nt(q2, kc_ref[g])
    pws, pcs, dens = [], [], []
    for ip in range(QROWS // 2):
        r0 = QROWS * rb + 2 * ip
        rs0 = jnp.clip(r0 - WIN_ROWS // 2, 0, n_rows - WIN_ROWS)
        s = jnp.minimum(rs0 - base, KROWS - SLAB_ROWS)
        typ = jnp.where(r0 < 2, 1, jnp.where(r0 < 4, 2, jnp.where(r0 == n_rows - 4, 3, jnp.where(r0 == n_rows - 2, 4, 0))))
        off = pl.multiple_of(s * GRID_W, GRID_W)
        kw = k_ref[0, pl.ds(off, slab), :]
        ra, rbb = slice(ip * pr, (ip + 1) * pr), slice(tq + ip * pr, tq + (ip + 1) * pr)
        qq = jnp.concatenate([q2[ra], q2[rbb]], axis=0)
        sw = _dot_nt(qq, kw) + jnp.concatenate([sb_ref[2 * g, typ], sb_ref[2 * g + 1, typ]], axis=0)
        sc = jnp.concatenate([sc_all[ra], sc_all[rbb]], axis=0)
        m = jnp.maximum(jnp.max(sw, axis=-1, keepdims=True), jnp.max(sc, axis=-1, keepdims=True))
        pw = jnp.exp(sw - m)
        pc = jnp.exp(sc - m)
        dens.append(jnp.sum(pw, axis=-1, keepdims=True) + jnp.sum(pc, axis=-1, keepdims=True))
        pws.append((pw.astype(bf16), off))
        pcs.append(pc.astype(bf16))
    vt_pt = lambda v, p: lax.dot_general(v, p, (((0,), (1,)), ((), ())), preferred_element_type=f32)
    oc_t = vt_pt(vc_ref[g], jnp.concatenate(pcs, axis=0))
    lane_p = lax.broadcasted_iota(i32, (pr, LANES), 1)
    for ip in range(QROWS // 2):
        pw, off = pws[ip]
        o_t = vt_pt(v_ref[0, pl.ds(off, slab), :], pw) + oc_t[:, ip * 2 * pr:(ip + 1) * 2 * pr]
        o = o_t.T / dens[ip]
        o_ref[g, ip * pr:(ip + 1) * pr, :] = jnp.where(lane_p < HEAD_DIM, o[:pr], o[pr:]).astype(bf16)


def _attention(qkv, ckv, sb, batch, seq):
    n_rows = seq // GRID_W
    assert n_rows % QROWS == 0 and n_rows >= KROWS
    nrb = n_rows // QROWS
    tq = QROWS * GRID_W
    tk = KROWS * GRID_W
    n_tok = batch * seq

    def kv_map(which, g):
        def index(b, hg, rb):
            base = jnp.clip(QROWS * rb - WIN_ROWS // 2, 0, n_rows - KROWS)
            return (which * N_PAIRS + hg * ATTN_PAIRS + g, (b * n_rows + base) * GRID_W, 0)
        return index

    kv_specs = lambda which: [pl.BlockSpec((pl.Element(1), pl.Element(tk), pl.Element(LANES)), kv_map(which, g))
                              for g in range(ATTN_PAIRS)]
    n_groups = N_PAIRS // ATTN_PAIRS
    ctx_spec = lambda which: pl.BlockSpec((ATTN_PAIRS, CTX_LEN, LANES),
                                          lambda b, hg, rb: (which * n_groups + hg, b, 0))
    return pl.pallas_call(
        functools.partial(_attn_kernel, n_rows),
        grid=(batch, n_groups, nrb),
        in_specs=[pl.BlockSpec((ATTN_PAIRS, tq, LANES), lambda b, hg, rb: (hg, b * nrb + rb, 0))]
        + kv_specs(1) + kv_specs(2) + [
            ctx_spec(0), ctx_spec(1),
            pl.BlockSpec((2 * ATTN_PAIRS, N_SLAB_TYPES, 2 * GRID_W, SLAB_ROWS * GRID_W), lambda b, hg, rb: (hg, 0, 0, 0)),
        ],
        out_specs=pl.BlockSpec((ATTN_PAIRS, tq, LANES), lambda b, hg, rb: (hg, b * nrb + rb, 0)),
        out_shape=jax.ShapeDtypeStruct((N_PAIRS, n_tok, LANES), bf16),
        compiler_params=_cparams(3),
        name="attn",
    )(qkv, *([qkv] * (2 * ATTN_PAIRS)), ckv, ckv, sb)


A_CHUNK = 8


def _filter_kernel(seq, w1t_ref, w1c_ref, w1s_ref, b1_ref, fr1_ref, w2t_ref, b2_ref, fr2_ref, w3t_ref, dl_ref,
                   fq_ref, o_ref):
    st = pl.program_id(0)
    npos = A_CHUNK * LANES
    n = st * npos + lax.broadcasted_iota(i32, (1, npos), 1)
    idx = jnp.where(n < seq, n, 2 * seq - n).astype(f32)
    t = idx / (seq - 1)
    ang = (2.0 * math.pi / seq) * idx
    fa = fq_ref[...] * ang
    z1 = w1t_ref[...] * t + _dot(w1c_ref[...], jnp.cos(fa), HI) + _dot(w1s_ref[...], -jnp.sin(fa), HI) + b1_ref[...]
    h1 = jnp.sin(fr1_ref[...] * z1)
    h2 = jnp.sin(fr2_ref[...] * (_dot(w2t_ref[...], h1, HI) + b2_ref[...]))
    filt = _dot(w3t_ref[0], h2, SPLIT3) * jnp.exp(-t * dl_ref[...])
    filt = jnp.where(n == seq, 0.0, filt)
    for al in range(A_CHUNK):
        o_ref[0, pl.ds(al, D_B, stride=A_CHUNK), :] = filt[:, al * LANES:(al + 1) * LANES]


def _from_chunks(blk, db):
    cat = lambda d: jnp.concatenate([blk[c, d * A_CHUNK:(d + 1) * A_CHUNK] for c in range(blk.shape[0])], axis=0)
    return jnp.stack([cat(d) for d in range(db)])


def _to_chunks(y):
    cat = lambda c: jnp.concatenate([y[d, c * A_CHUNK:(c + 1) * A_CHUNK] for d in range(y.shape[0])], axis=0)
    return jnp.stack([cat(c) for c in range(y.shape[1] // A_CHUNK)])


def _hyena_filter(seq, w1, b1, fr1, w2, b2, fr2, w3):
    na = 2 * seq // LANES
    steps = na // A_CHUNK
    fwd_steps = steps // 2
    col = lambda v: v.reshape(-1, 1).astype(f32)
    deltas = jnp.abs(jnp.linspace(math.log(DECAY_TARGET) / SLOW_DECAY_PCT, math.log(DECAY_TARGET) / FAST_DECAY_PCT,
                                  D_B, dtype=f32))
    freqs = jnp.linspace(1e-4, N_BANDS - 1, N_BANDS, dtype=f32)
    w3t = w3.astype(f32).T.reshape(2, D_B, FILT_HID)
    w1 = w1.astype(f32)
    args = (w1[0:1].T, w1[1:1 + N_BANDS].T, w1[1 + N_BANDS:].T, col(b1), col(fr1), w2.astype(f32).T, col(b2), col(fr2),
            w3t, col(deltas), col(freqs))
    full = lambda a: pl.BlockSpec(a.shape, lambda s: (0,) * a.ndim)
    in_specs = [full(a) for a in args]
    in_specs[8] = pl.BlockSpec((1, D_B, FILT_HID), lambda s: (s // fwd_steps, 0, 0))
    return pl.pallas_call(
        functools.partial(_filter_kernel, seq),
        grid=(steps,),
        in_specs=in_specs,
        out_specs=pl.BlockSpec((1, D_B * A_CHUNK, LANES), lambda s: (s, 0, 0)),
        out_shape=jax.ShapeDtypeStruct((steps, D_B * A_CHUNK, LANES), f32),
        compiler_params=_cparams(1),
        name="filt",
    )(*args)


def _dft_consts(seq):
    n = 2 * seq
    n1 = n // LANES
    a_used = seq // LANES
    k1 = np.arange(n1)[:, None]
    a = np.arange(n1)[None, :]
    ang1 = -2.0 * np.pi * ((k1 * a) % n1) / n1
    f1 = np.concatenate([np.cos(ang1), np.sin(ang1)], axis=0)
    b = np.arange(LANES)[None, :]
    angt = -2.0 * np.pi * ((k1 * b) % n) / n
    tw_r, tw_i = np.cos(angt), np.sin(angt)
    bb = np.arange(LANES)[:, None]
    k2 = np.arange(LANES)[None, :]
    angg = -2.0 * np.pi * ((bb * k2) % LANES) / LANES
    gr, gi = np.cos(angg), np.sin(angg)
    g_fwd = np.block([[gr, gi], [-gi, gr]])
    g_inv = np.block([[gr, -gi], [gi, gr]])
    f1_inv = np.concatenate([np.cos(ang1).T[:a_used], np.sin(ang1).T[:a_used]], axis=1)
    c = lambda x: jnp.asarray(x, f32)
    return dict(n1=n1, a_used=a_used, f1_full=c(f1), f1=c(f1[:, :a_used]), tw_r=c(tw_r), tw_i=c(tw_i), g_fwd=c(g_fwd),
                g_inv=c(g_inv), f1_inv=c(f1_inv))


def _pair_dot(m, x, precision=None):
    out = []
    for p in range(x.shape[0] // 2):
        y2 = _dot(m, jnp.concatenate([x[2 * p], x[2 * p + 1]], axis=1), precision)
        out += [y2[:, :LANES], y2[:, LANES:]]
    return jnp.stack(out)


def _fft_fwd(x, f1, tw_r, tw_i, g_fwd, precision, cast):
    db = x.shape[0]
    n1 = tw_r.shape[0]
    y = _pair_dot(cast(f1), cast(x), precision)
    yr, yi = y[:, :n1], y[:, n1:]
    ytr = yr * tw_r - yi * tw_i
    yti = yr * tw_i + yi * tw_r
    lhs = jnp.concatenate([ytr, yti], axis=-1).reshape(db * n1, 2 * LANES)
    return _dot(cast(lhs), cast(g_fwd), precision)


def _fftk_kernel(x_ref, f1_ref, twr_ref, twi_ref, g_ref, o_ref):
    db, n1 = o_ref.shape[0], twr_ref.shape[0]
    x = _from_chunks(x_ref[...], db)
    z = _fft_fwd(x, f1_ref[...], twr_ref[...], twi_ref[...], g_ref[...], SPLIT3, lambda v: v)
    o_ref[...] = z.reshape(db, n1, 2 * LANES)


def _filter_spectrum(kf, dc):
    n_chunks = kf.shape[0]
    d, n1 = kf.shape[1] // A_CHUNK, n_chunks * A_CHUNK
    db = 16
    full = lambda a: pl.BlockSpec(a.shape, lambda i: (0,) * a.ndim)
    consts = (dc["f1_full"], dc["tw_r"], dc["tw_i"], dc["g_fwd"])
    return pl.pallas_call(
        _fftk_kernel,
        grid=(d // db,),
        in_specs=[pl.BlockSpec((n_chunks, db * A_CHUNK, LANES), lambda i: (0, i, 0))] + [full(a) for a in consts],
        out_specs=pl.BlockSpec((db, n1, 2 * LANES), lambda i: (i, 0, 0)),
        out_shape=jax.ShapeDtypeStruct((d, n1, 2 * LANES), f32),
        compiler_params=_cparams(1),
        name="fftk",
    )(kf, *consts)


def _hpre_kernel(n_chunks, u_ref, up_ref, un_ref, cw_ref, cb_ref, zt_ref, x0_ref):
    ac = pl.program_id(1)
    tm = u_ref.shape[0]
    has_prev = (ac > 0).astype(f32)
    has_next = (ac < n_chunks - 1).astype(f32)
    row = lax.broadcasted_iota(i32, (tm, LANES), 0)
    halo = up_ref.shape[0]

    def conv(c0):
        sl = slice(c0, c0 + LANES)
        u = u_ref[:, sl].astype(f32)
        prev = up_ref[halo - 1:halo, sl].astype(f32) * has_prev
        nxt = un_ref[0:1, sl].astype(f32) * has_next
        um = jnp.where(row == 0, prev, pltpu.roll(u, 1, 0))
        upl = jnp.where(row == tm - 1, nxt, pltpu.roll(u, tm - 1, 0))
        return um * cw_ref[0:1, sl] + u * cw_ref[1:2, sl] + upl * cw_ref[2:3, sl] + cb_ref[:, sl]

    for dt in range(D_B // LANES):
        v = conv(dt * LANES)
        x1 = conv(D_B + dt * LANES)
        x0_ref[:, dt * LANES:(dt + 1) * LANES] = conv(2 * D_B + dt * LANES).astype(bf16)
        z = v * x1
        for al in range(A_CHUNK):
            rows = pl.ds(dt * LANES * A_CHUNK + al, LANES, stride=A_CHUNK)
            zt_ref[0, 0, rows, :] = z[al * LANES:(al + 1) * LANES, :].T


def _hyena_pre(u, conv_w, conv_b, batch, seq):
    n = u.shape[0]
    tm = A_CHUNK * LANES
    n_chunks = seq // tm
    halo = 16
    hb = tm // halo
    cw = jnp.concatenate([conv_w.astype(f32), jnp.zeros((8 - conv_w.shape[0], conv_w.shape[1]), f32)], axis=0)
    return pl.pallas_call(
        functools.partial(_hpre_kernel, n_chunks),
        grid=(batch, n_chunks),
        in_specs=[
            pl.BlockSpec((tm, 3 * D_B), lambda b, c: (b * n_chunks + c, 0)),
            pl.BlockSpec((halo, 3 * D_B), lambda b, c: (jnp.maximum((b * n_chunks + c) * hb - 1, 0), 0)),
            pl.BlockSpec((halo, 3 * D_B), lambda b, c: (jnp.minimum((b * n_chunks + c + 1) * hb, n // halo - 1), 0)),
            pl.BlockSpec((8, 3 * D_B), lambda b, c: (0, 0)),
            pl.BlockSpec((1, 3 * D_B), lambda b, c: (0, 0)),
        ],
        out_specs=[
            pl.BlockSpec((1, 1, D_B * A_CHUNK, LANES), lambda b, c: (b, c, 0, 0)),
            pl.BlockSpec((tm, D_B), lambda b, c: (b * n_chunks + c, 0)),
        ],
        out_shape=[
            jax.ShapeDtypeStruct((batch, n_chunks, D_B * A_CHUNK, LANES), f32),
            jax.ShapeDtypeStruct((n, D_B), bf16),
        ],
        compiler_params=_cparams(2),
        name="hpre",
    )(u, u, u, cw, conv_b.reshape(1, -1).astype(f32))


def _fftconv_kernel(z_ref, kh_ref, skip_ref, f1_ref, twr_ref, twi_ref, gf_ref, gi_ref, f1i_ref, y_ref):
    db = skip_ref.shape[0]
    n1 = twr_ref.shape[0]
    tw_r, tw_i = twr_ref[...], twi_ref[...]
    cast = lambda v: v.astype(bf16)
    z = _from_chunks(z_ref[0], db)
    spec = _fft_fwd(z, f1_ref[...], tw_r, tw_i, gf_ref[...], None, cast)
    kh = kh_ref[...].reshape(db * n1, 2 * LANES)
    sr, si = spec[:, :LANES], spec[:, LANES:]
    kr, ki = kh[:, :LANES], kh[:, LANES:]
    prod = jnp.concatenate([sr * kr - si * ki, sr * ki + si * kr], axis=-1)
    q = _dot(cast(prod), cast(gi_ref[...])).reshape(db, n1, 2 * LANES)
    qr, qi = q[:, :, :LANES], q[:, :, LANES:]
    rhs = jnp.concatenate([qr * tw_r + qi * tw_i, qi * tw_r - qr * tw_i], axis=1)
    y = _pair_dot(cast(f1i_ref[...]), cast(rhs)) * (1.0 / (n1 * LANES))
    y_ref[0] = _to_chunks(y + z * skip_ref[...])


def _fftconv(zt, khat, skip, dc):
    batch, n_chunks, rows, _ = zt.shape
    d = rows // A_CHUNK
    n1 = dc["n1"]
    db = 64
    consts = (dc["f1"], dc["tw_r"], dc["tw_i"], dc["g_fwd"], dc["g_inv"], dc["f1_inv"])
    full = lambda a: pl.BlockSpec(a.shape, lambda i, b: (0,) * a.ndim)
    blk = lambda: pl.BlockSpec((1, n_chunks, db * A_CHUNK, LANES), lambda i, b: (b, 0, i, 0))
    return pl.pallas_call(
        _fftconv_kernel,
        grid=(d // db, batch),
        in_specs=[
            blk(),
            pl.BlockSpec((db, n1, 2 * LANES), lambda i, b: (i, 0, 0)),
            pl.BlockSpec((db, 1, 1), lambda i, b: (i, 0, 0)),
        ] + [full(a) for a in consts],
        out_specs=blk(),
        out_shape=jax.ShapeDtypeStruct(zt.shape, f32),
        compiler_params=_cparams(2),
        name="fftconv",
    )(zt, khat, skip.reshape(d, 1, 1).astype(f32), *consts)


IDX_ROWS = 8


def _merge_kernel(oa_ref, yc_ref, x0_ref, ga_ref, gb_ref, x_ref, g1_ref, wba_ref, wbb_ref, wout_ref, n2g_ref, sh2_ref,
                  sc2_ref, wrt_ref, br_ref, x1_ref, h2_ref, idx_ref, wt_ref, ob_scr):
    a_tile = x0_ref.shape[0] // LANES
    part = pl.program_id(0) % (A_CHUNK // a_tile)
    for dt in range(D_B // LANES):
        for al in range(a_tile):
            rows = slice(al * LANES, (al + 1) * LANES)
            cols = slice(dt * LANES, (dt + 1) * LANES)
            src = pl.ds(dt * LANES * A_CHUNK + part * a_tile + al, LANES, stride=A_CHUNK)
            ob_scr[rows, cols] = (x0_ref[rows, cols].astype(f32) * yc_ref[0, 0, src, :].T).astype(bf16)
    oa = jnp.concatenate([oa_ref[hp] for hp in range(N_PAIRS)], axis=1)
    m = ga_ref[...].astype(f32) * _dot(oa, wba_ref[...]) + gb_ref[...].astype(f32) * _dot(ob_scr[...], wbb_ref[...])
    x1 = x_ref[...] + g1_ref[0] * _dot(m.astype(bf16), wout_ref[...])
    x1_ref[...] = x1
    ms = jnp.mean(x1 * x1, axis=-1, keepdims=True)
    h2 = (x1 * lax.rsqrt(ms + RMS_EPS) * n2g_ref[...]) * (1.0 + sc2_ref[0]) + sh2_ref[0]
    h_hi = h2.astype(bf16)
    h2_ref[...] = h_hi
    h_lo = (h2 - h_hi.astype(f32)).astype(bf16)
    wr = wrt_ref[...]
    w_hi = wr.astype(bf16)
    w_lo = (wr - w_hi.astype(f32)).astype(bf16)
    logits = _dot_nt(w_hi, h_hi) + _dot_nt(w_hi, h_lo) + _dot_nt(w_lo, h_hi) + br_ref[...]
    eio = lax.broadcasted_iota(i32, logits.shape, 0)
    vals = logits
    idxs, tops = [], []
    for _ in range(TOP_K):
        mx = jnp.max(vals, axis=0, keepdims=True)
        ix = jnp.min(jnp.where(vals == mx, eio, N_EXPERTS), axis=0, keepdims=True)
        idxs.append(ix)
        tops.append(mx)
        vals = jnp.where(eio == ix, -jnp.inf, vals)
    ex = [jnp.exp(v - tops[0]) for v in tops]
    den = ex[0] + ex[1] + ex[2] + ex[3]
    idx_ref[...] = jnp.concatenate(idxs + [jnp.full_like(idxs[0], -1)] * (IDX_ROWS - TOP_K), axis=0)
    wt_ref[...] = jnp.concatenate([e / den for e in ex] + [jnp.zeros_like(den)] * (IDX_ROWS - TOP_K), axis=0)


def _merge(oa3, yc, x0, gates, x2, g1, wba, wbb, wout, n2g, sh2, sc2, wrt, br, tm, tiles_per_batch):
    n, d = x2.shape
    parts = A_CHUNK * LANES // tm
    tok = lambda col=0: pl.BlockSpec((tm, d), lambda i: (i, col))
    mod = lambda: pl.BlockSpec((1, 1, d), lambda i: (i // tiles_per_batch, 0, 0))
    wsp = lambda: pl.BlockSpec((d, d), lambda i: (0, 0))
    ga, gb = gates, gates
    return pl.pallas_call(
        _merge_kernel,
        grid=(n // tm,),
        in_specs=[
            pl.BlockSpec((N_PAIRS, tm, LANES), lambda i: (0, i, 0)),
            pl.BlockSpec((1, 1, D_B * A_CHUNK, LANES),
                         lambda i: (i // tiles_per_batch, (i % tiles_per_batch) // parts, 0, 0)),
            tok(), tok(0), tok(1), tok(), mod(), wsp(), wsp(), wsp(),
            pl.BlockSpec((1, d), lambda i: (0, 0)), mod(), mod(),
            pl.BlockSpec((N_EXPERTS, d), lambda i: (0, 0)),
            pl.BlockSpec((N_EXPERTS, 1), lambda i: (0, 0)),
        ],
        out_specs=[tok(), tok(), pl.BlockSpec((IDX_ROWS, tm), lambda i: (0, i)), pl.BlockSpec((IDX_ROWS, tm), lambda i: (0, i))],
        out_shape=[
            jax.ShapeDtypeStruct((n, d), f32),
            jax.ShapeDtypeStruct((n, d), bf16),
            jax.ShapeDtypeStruct((IDX_ROWS, n), i32),
            jax.ShapeDtypeStruct((IDX_ROWS, n), f32),
        ],
        scratch_shapes=[pltpu.VMEM((tm, d), bf16)],
        compiler_params=_cparams(1),
        name="merge",
    )(oa3, yc, x0, ga, gb, x2, g1, wba, wbb, wout, n2g, sh2, sc2, wrt, br)


MOE_SUB = 1024
MOE_NSUB = 2
MOE_TILE = MOE_SUB * MOE_NSUB
MOE_ROWS = 160
MOE_CAP = 12 * MOE_ROWS


def _moe_kernel(h2_ref, idx_ref, wt_ref, wgu_ref, bgu_ref, wd_ref, bd_ref, x1_ref, g2_ref, y_ref, rank_scr, msk_scr,
                wte_scr, o_scr, p_scr, st_ref, cnt_ref):
    e = pl.program_id(1)
    ts = MOE_SUB
    last = e == N_EXPERTS - 1

    @pl.when(e == 0)
    def _():
        y_ref[...] = jnp.zeros_like(y_ref)
        o_scr[...] = jnp.zeros_like(o_scr)
        p_scr[...] = jnp.zeros_like(p_scr)
        eio = lax.broadcasted_iota(i32, (N_EXPERTS, ts), 0)
        lane = lax.broadcasted_iota(i32, (N_EXPERTS, ts), 1)

        for s in range(MOE_NSUB):
            idx = idx_ref[s]
            w = wt_ref[s]
            msk = jnp.zeros((N_EXPERTS, ts), f32)
            wte = jnp.zeros((N_EXPERTS, ts), f32)
            for k in range(TOP_K):
                hit = idx[k:k + 1, :] == eio
                msk = msk + hit.astype(f32)
                wte = wte + jnp.where(hit, w[k:k + 1, :], 0.0)
            csum = msk
            sh = 1
            while sh < ts:
                csum = csum + jnp.where(lane >= sh, pltpu.roll(csum, sh, 1), 0.0)
                sh *= 2
            rank_scr[s] = (csum - msk).astype(i32)
            msk_scr[s] = msk
            wte_scr[s] = wte
            st_ref[s] = 0
            for ex in range(N_EXPERTS):
                cnt_ref[s * N_EXPERTS + ex] = jnp.sum(csum[ex:ex + 1, ts - 1:ts]).astype(i32)

    subs = range(MOE_NSUB)
    toks = [slice(s * ts, (s + 1) * ts) for s in subs]
    ranks = [rank_scr[s, pl.ds(e, 1), :] for s in subs]
    sels = [msk_scr[s, pl.ds(e, 1), :] > 0.0 for s in subs]
    wtes = [wte_scr[s, pl.ds(e, 1), :] for s in subs]
    n_chunks = [(cnt_ref[s * N_EXPERTS + e] + MOE_ROWS - 1) // MOE_ROWS for s in subs]
    max_chunks = functools.reduce(jnp.maximum, n_chunks)

    def combine(s):
        y_ref[toks[s], :] += _dot_tn(p_scr[s], o_scr[s])
        o_scr[s] = jnp.zeros((MOE_CAP, D_MODEL), bf16)
        st_ref[s] = 0

    def trip(jc, carry):
        live = [jc < n for n in n_chunks]
        for s in subs:
            pl.when((live[s] & (st_ref[s] + MOE_ROWS > MOE_CAP)) | (jc >= max_chunks))(functools.partial(combine, s))

        @pl.when(jc < max_chunks)
        def _():
            jio = lax.broadcasted_iota(i32, (MOE_ROWS, ts), 0)
            hits = [((ranks[s] - jc * MOE_ROWS) == jio) & sels[s] for s in subs]
            onehots = [h.astype(bf16) for h in hits]
            xg = jnp.concatenate([_dot(onehots[s], h2_ref[toks[s], :]).astype(bf16) for s in subs], axis=0)
            gu = _dot(xg, wgu_ref[0]) + bgu_ref[0]
            gate = jnp.minimum(gu[:, :D_FF], SWIGLU_LIMIT)
            up = jnp.clip(gu[:, D_FF:], -SWIGLU_LIMIT, SWIGLU_LIMIT)
            act = (up + 1.0) * gate * _sigmoid(SWIGLU_ALPHA * gate)
            o = _dot(act.astype(bf16), wd_ref[0]) + bd_ref[0]
            for s in subs:

                @pl.when(live[s])
                def _(s=s):
                    fill = pl.multiple_of(st_ref[s], 16)
                    wrow = jnp.sum(jnp.where(hits[s], wtes[s], 0.0), axis=1, keepdims=True)
                    o_scr[s, pl.ds(fill, MOE_ROWS), :] = (o[s * MOE_ROWS:(s + 1) * MOE_ROWS] * wrow).astype(bf16)
                    p_scr[s, pl.ds(fill, MOE_ROWS), :] = onehots[s]
                    st_ref[s] = fill + MOE_ROWS

        return carry

    lax.fori_loop(0, max_chunks + last.astype(i32), trip, 0)

    @pl.when(last)
    def _():
        y_ref[...] = x1_ref[...] + g2_ref[0] * y_ref[...]


def _moe(h2, idx, wt, wgu, bgu, wd, bd, x1, g2, tiles_per_batch):
    n, d = h2.shape
    t = MOE_TILE
    once = dict(pipeline_mode=pl.Buffered(1))
    per_sub = lambda a: jnp.transpose(a.reshape(IDX_ROWS, n // MOE_SUB, MOE_SUB), (1, 0, 2))
    route_spec = lambda: pl.BlockSpec((MOE_NSUB, IDX_ROWS, MOE_SUB), lambda i, e: (i, 0, 0), **once)
    return pl.pallas_call(
        _moe_kernel,
        grid=(n // t, N_EXPERTS),
        in_specs=[
            pl.BlockSpec((t, d), lambda i, e: (i, 0), **once),
            route_spec(), route_spec(),
            pl.BlockSpec((1, d, 2 * D_FF), lambda i, e: (e, 0, 0)),
            pl.BlockSpec((1, 1, 2 * D_FF), lambda i, e: (e, 0, 0)),
            pl.BlockSpec((1, D_FF, d), lambda i, e: (e, 0, 0)),
            pl.BlockSpec((1, 1, d), lambda i, e: (e, 0, 0)),
            pl.BlockSpec((t, d), lambda i, e: (i, 0), **once),
            pl.BlockSpec((1, 1, d), lambda i, e: (i // tiles_per_batch, 0, 0)),
        ],
        out_specs=pl.BlockSpec((t, d), lambda i, e: (i, 0), **once),
        out_shape=jax.ShapeDtypeStruct((n, d), f32),
        scratch_shapes=[
            pltpu.VMEM((MOE_NSUB, N_EXPERTS, MOE_SUB), i32),
            pltpu.VMEM((MOE_NSUB, N_EXPERTS, MOE_SUB), f32),
            pltpu.VMEM((MOE_NSUB, N_EXPERTS, MOE_SUB), f32),
            pltpu.VMEM((MOE_NSUB, MOE_CAP, d), bf16),
            pltpu.VMEM((MOE_NSUB, MOE_CAP, MOE_SUB), bf16),
            pltpu.SMEM((MOE_NSUB,), i32),
            pltpu.SMEM((MOE_NSUB * N_EXPERTS,), i32),
        ],
        compiler_params=_cparams(2),
        name="moe",
    )(h2, per_sub(idx), per_sub(wt), wgu, bgu, wd, bd, x1, g2)


def _layer(x, c, ctx, c_ctx, w_ada, b_ada, norm1_g, norm2_g, w_in, q_norm_g, k_norm_g, rpb, conv_w, conv_b, filt_w1,
           filt_b1, filt_freq1, filt_w2, filt_b2, filt_freq2, filt_w3, hyena_skip, w_branch_a, w_branch_b, w_out,
           w_router, b_router, w_gate_up, b_gate_up, w_down, b_down):
    batch, seq, d = x.shape
    n = batch * seq
    n_ctx = ctx.shape[1]
    assert d == D_MODEL and n_ctx == CTX_LEN and seq % (A_CHUNK * LANES) == 0 and seq % MOE_TILE == 0

    rows = 8 * ((batch + 1 + 7) // 8)
    cv = jnp.zeros((rows, d), f32).at[:batch].set(c).at[batch].set(c_ctx)
    mod = _adaln(cv, w_ada, b_ada.reshape(1, -1))
    sh1, sc1, g1, sh2, sc2, g2 = [mod[:batch, i * d:(i + 1) * d].reshape(batch, 1, d) for i in range(N_MOD)]
    modc = jnp.broadcast_to(mod[batch:batch + 1], (batch, N_MOD * d))
    sh1c, sc1c = modc[:, :d].reshape(batch, 1, d), modc[:, d:2 * d].reshape(batch, 1, d)

    w_in_bf = w_in.astype(bf16)
    ones = jnp.asarray(np.kron(np.eye(2), np.ones((HEAD_DIM, HEAD_DIM))), bf16)
    qg = (jnp.tile(q_norm_g.astype(f32), N_HEADS) * (HEAD_DIM ** -0.5)).reshape(1, d)
    kg = jnp.tile(k_norm_g.astype(f32), N_HEADS).reshape(1, d)
    n1g = norm1_g.reshape(1, d)

    (ckv,) = _inproj(ctx.reshape(batch * n_ctx, d), n1g, sh1c, sc1c, w_in_bf, (kg,), ones, 2, 0, 0, 1, n_ctx, 1)
    tm = 512
    x2 = x.reshape(n, d)
    tm_in = 1024
    qkv, u, gates = _inproj(x2, n1g, sh1, sc1, w_in_bf, (qg, kg), ones, 3, 3, 2, 0, tm_in, seq // tm_in)

    oa3 = _attention(qkv, ckv, _slab_bias_tables(rpb), batch, seq)

    dc = _dft_consts(seq)
    kf = _hyena_filter(seq, filt_w1, filt_b1, filt_freq1, filt_w2, filt_b2, filt_freq2, filt_w3)
    khat = _filter_spectrum(kf, dc)
    zt, x0 = _hyena_pre(u, conv_w, conv_b, batch, seq)
    yc = _fftconv(zt, khat, hyena_skip, dc)

    x1, h2, idx, wt = _merge(oa3, yc, x0, gates, x2, g1, w_branch_a.astype(bf16), w_branch_b.astype(bf16),
                             w_out.astype(bf16), norm2_g.reshape(1, d), sh2, sc2, w_router.astype(f32).T,
                             b_router.reshape(-1, 1).astype(f32), tm, seq // tm)
    out = _moe(h2, idx, wt, w_gate_up.astype(bf16), b_gate_up.reshape(N_EXPERTS, 1, -1).astype(f32),
               w_down.astype(bf16), b_down.reshape(N_EXPERTS, 1, -1).astype(f32), x1, g2, seq // MOE_TILE)
    return out.reshape(batch, seq, d)


def kernel(x, c, ctx, c_ctx, w_ada, b_ada, norm1_g, norm2_g, w_in, q_norm_g, k_norm_g, rpb, conv_w, conv_b, filt_w1, filt_b1, filt_freq1, filt_w2, filt_b2, filt_freq2, filt_w3, hyena_skip, w_branch_a, w_branch_b, w_out, w_router, b_router, w_gate_up, b_gate_up, w_down, b_down):
    assert w_ada.shape[0] == 1, "single-layer stack"
    return _layer(x, c, ctx, c_ctx, w_ada[0], b_ada[0], norm1_g[0], norm2_g[0], w_in[0], q_norm_g[0], k_norm_g[0],
                  rpb[0], conv_w[0], conv_b[0], filt_w1[0], filt_b1[0], filt_freq1[0], filt_w2[0], filt_b2[0],
                  filt_freq2[0], filt_w3[0], hyena_skip[0], w_branch_a[0], w_branch_b[0], w_out[0], w_router[0],
                  b_router[0], w_gate_up[0], b_gate_up[0], w_down[0], b_down[0])
```
